```python
import jax, jax.numpy as jnp
from jax import lax
import numpy as np

D_MODEL = 1024
BATCH = 4
SEQ = 8192
DEPTH = 2

CHUNK = 64
Q_BLOCK = 128
CONV_WIDTH = 31
N_HEADS = 16
HEAD_DIM = D_MODEL // N_HEADS
ATTN_WIDTH = N_HEADS * HEAD_DIM
D_FF = -(-(8 * D_MODEL) // (3 * 256)) * 256
N_A_LAYERS = DEPTH // 2
N_B_LAYERS = DEPTH - N_A_LAYERS
EPS = 1e-6
FORGET_BIAS_MEAN = 2.0

kernel_name = "yoco_conformer_fox_adaln_trunk"


def _rmsnorm(x, g):
    x32 = x.astype(jnp.float32)
    y = x32 * lax.rsqrt(jnp.mean(x32 * x32, axis=-1, keepdims=True) + EPS)
    return (y * g.astype(jnp.float32)).astype(x.dtype)


def _layernorm(x, g, b):
    x32 = x.astype(jnp.float32)
    mu = jnp.mean(x32, axis=-1, keepdims=True)
    var = jnp.mean(jnp.square(x32 - mu), axis=-1, keepdims=True)
    y = (x32 - mu) * lax.rsqrt(var + EPS)
    return (y * g.astype(jnp.float32) + b.astype(jnp.float32)).astype(x.dtype)


def _modulate(h, shift, scale):
    return h * (1.0 + scale[:, None, :]) + shift[:, None, :]


def _conformer_conv(h, w_in, b_in, w_dw, b_dw, ln_g, ln_b, w_out, b_out):
    u = h @ w_in + b_in
    a, g = jnp.split(u, 2, axis=-1)
    u = a * jax.nn.sigmoid(g)
    d = u.shape[-1]
    u = lax.conv_general_dilated(
        u, w_dw[:, None, :].astype(u.dtype),
        window_strides=(1,), padding=[(CONV_WIDTH - 1, 0)],
        dimension_numbers=("NWC", "WIO", "NWC"),
        feature_group_count=d) + b_dw
    u = jax.nn.silu(_layernorm(u, ln_g, ln_b))
    return u @ w_out + b_out


def _forgetting_attention(q, k, v, cum):
    b, s, h, hd = q.shape
    n_blk = s // Q_BLOCK
    scale = hd ** -0.5
    qb = q.reshape(b, n_blk, Q_BLOCK, h, hd).transpose(1, 0, 2, 3, 4)
    cum_h = cum.transpose(0, 2, 1)
    cqb = cum_h.reshape(b, h, n_blk, Q_BLOCK).transpose(2, 0, 1, 3)
    key_pos = jnp.arange(s)

    def one_block(args):
        q_i, cq_i, i = args
        logits = jnp.einsum("bqhd,bkhd->bhqk", q_i, k).astype(jnp.float32) * scale
        logits = logits + (cq_i[..., :, None] - cum_h[:, :, None, :])
        q_pos = i * Q_BLOCK + jnp.arange(Q_BLOCK)
        mask = key_pos[None, :] <= q_pos[:, None]
        logits = jnp.where(mask[None, None], logits, -jnp.inf)
        p = jax.nn.softmax(logits, axis=-1)
        return jnp.einsum("bhqk,bkhd->bqhd", p.astype(v.dtype), v)

    out = lax.map(one_block, (qb, cqb, jnp.arange(n_blk)))
    return out.transpose(1, 0, 2, 3, 4).reshape(b, s, h, hd)


def setup_inputs(seed: int = 0) -> dict:
    key = jax.random.key(seed)
    ks = iter(jax.random.split(key, 32))
    D, F, K, H = D_MODEL, D_FF, CONV_WIDTH, N_HEADS

    def nrm(shape, std):
        return jax.random.normal(next(ks), shape, jnp.float32) * std

    def gain(shape):
        return 1.0 + nrm(shape, 0.02)

    return {
        "x": nrm((BATCH, SEQ, D), 1.0),
        "c": nrm((BATCH, D), 1.0),
        "mix_norm_g": gain((DEPTH, D)),
        "mix_ada_w": nrm((DEPTH, D, 3 * D), 0.5 * D ** -0.5),
        "mix_ada_b": nrm((DEPTH, 3 * D), 0.02),
        "ffn_norm_g": gain((DEPTH, D)),
        "ffn_ada_w": nrm((DEPTH, D, 3 * D), 0.5 * D ** -0.5),
        "ffn_ada_b": nrm((DEPTH, 3 * D), 0.02),
        "ffn_w_in": nrm((DEPTH, D, 2 * F), D ** -0.5),
        "ffn_w_out": nrm((DEPTH, F, D), F ** -0.5),
        "conv_w_in": nrm((N_A_LAYERS, D, 2 * D), D ** -0.5),
        "conv_b_in": nrm((N_A_LAYERS, 2 * D), 0.02),
        "conv_w_dw": nrm((N_A_LAYERS, K, D), K ** -0.5),
        "conv_b_dw": nrm((N_A_LAYERS, D), 0.02),
        "conv_ln_g": gain((N_A_LAYERS, D)),
        "conv_ln_b": nrm((N_A_LAYERS, D), 0.02),
        "conv_w_out": nrm((N_A_LAYERS, D, D), D ** -0.5),
        "conv_b_out": nrm((N_A_LAYERS, D), 0.02),
        "kv_norm_g": gain((D,)),
        "kv_ada_w": nrm((D, 2 * D), 0.5 * D ** -0.5),
        "kv_ada_b": nrm((2 * D,), 0.02),
        "kv_w": nrm((D, 2 * ATTN_WIDTH + H), D ** -0.5),
        "forget_b": FORGET_BIAS_MEAN + nrm((H,), 0.1),
        "attn_w_q": nrm((N_B_LAYERS, D, ATTN_WIDTH), D ** -0.5),
        "attn_w_o": nrm((N_B_LAYERS, ATTN_WIDTH, D), ATTN_WIDTH ** -0.5),
        "final_norm_g": gain((D,)),
    }


def reference(x, c, mix_norm_g, mix_ada_w, mix_ada_b, ffn_norm_g, ffn_ada_w, ffn_ada_b,
              ffn_w_in, ffn_w_out, conv_w_in, conv_b_in, conv_w_dw, conv_b_dw, conv_ln_g,
              conv_ln_b, conv_w_out, conv_b_out, kv_norm_g, kv_ada_w, kv_ada_b, kv_w,
              forget_b, attn_w_q, attn_w_o, final_norm_g):
    b, s, _ = x.shape
    c_act = jax.nn.silu(c)

    def ada(w, bias, n):
        return jnp.split(c_act @ w + bias, n, axis=-1)

    k_sh = v_sh = cum_sh = None
    for layer in range(DEPTH):
        shift, scale, gate = ada(mix_ada_w[layer], mix_ada_b[layer], 3)
        h = _modulate(_rmsnorm(x, mix_norm_g[layer]), shift, scale)
        if layer < N_A_LAYERS:
            i = layer
            y = _conformer_conv(h, conv_w_in[i], conv_b_in[i], conv_w_dw[i], conv_b_dw[i],
                                conv_ln_g[i], conv_ln_b[i], conv_w_out[i], conv_b_out[i])
        else:
            j = layer - N_A_LAYERS
            q = (h @ attn_w_q[j]).reshape(b, s, N_HEADS, HEAD_DIM)
            o = _forgetting_attention(q, k_sh, v_sh, cum_sh)
            y = o.reshape(b, s, ATTN_WIDTH) @ attn_w_o[j]
        x = x + gate[:, None, :] * y

        shift, scale, gate = ada(ffn_ada_w[layer], ffn_ada_b[layer], 3)
        h = _modulate(_rmsnorm(x, ffn_norm_g[layer]), shift, scale)
        u_gate, u_up = jnp.split(h @ ffn_w_in[layer], 2, axis=-1)
        x = x + gate[:, None, :] * ((jax.nn.silu(u_gate) * u_up) @ ffn_w_out[layer])

        if layer == N_A_LAYERS - 1:
            shift, scale = ada(kv_ada_w, kv_ada_b, 2)
            hk = _modulate(_rmsnorm(x, kv_norm_g), shift, scale)
            kvf = hk @ kv_w
            k_sh = kvf[..., :ATTN_WIDTH].reshape(b, s, N_HEADS, HEAD_DIM)
            v_sh = kvf[..., ATTN_WIDTH:2 * ATTN_WIDTH].reshape(b, s, N_HEADS, HEAD_DIM)
            f_logit = (kvf[..., 2 * ATTN_WIDTH:] + forget_b).astype(jnp.float32)
            cum_sh = jnp.cumsum(jax.nn.log_sigmoid(f_logit), axis=1)

    return _rmsnorm(x, final_norm_g)
```

```python
import functools

import jax
import jax.numpy as jnp
from jax import lax
from jax.experimental import pallas as pl
from jax.experimental.pallas import tpu as pltpu

F32 = jnp.float32
BF16 = jnp.bfloat16

EPS = 1e-6
V7X_VMEM_LIMIT_BYTES = 56 * 1024 * 1024
LANES = 128
CONV_HALO = 32

_NT = (((1,), (1,)), ((), ()))


def _const_spec(shape):
    return pl.BlockSpec(shape, lambda *_: (0,) * len(shape), pipeline_mode=pl.Buffered(1))


def _rms(x):
    return x * lax.rsqrt(jnp.mean(x * x, axis=-1, keepdims=True) + EPS)


def _ada_kernel(c_ref, w_ref, b_ref, o_ref):
    c = c_ref[...]
    ca = c * jax.nn.sigmoid(c)
    o_ref[...] = jnp.dot(ca, w_ref[...], precision=lax.Precision.HIGHEST,
                         preferred_element_type=F32) + b_ref[...]


def _ada(c_pad, w, b, tn=512):
    n_l, d, n = w.shape
    rows = c_pad.shape[0]
    return pl.pallas_call(
        _ada_kernel,
        grid=(n_l, n // tn),
        in_specs=[
            pl.BlockSpec((rows, d), lambda l, j: (0, 0)),
            pl.BlockSpec((None, d, tn), lambda l, j: (l, 0, j)),
            pl.BlockSpec((None, 1, tn), lambda l, j: (l, 0, j)),
        ],
        out_specs=pl.BlockSpec((None, rows, tn), lambda l, j: (l, 0, j)),
        out_shape=jax.ShapeDtypeStruct((n_l, rows, n), F32),
        name="ada_proj",
    )(c_pad, w, b.reshape(n_l, 1, n))


def _conv_in_kernel(x_ref, g_ref, shift_ref, scale_ref, w_ref, b_ref, u_ref):
    h = (_rms(x_ref[...]) * g_ref[...]) * (1.0 + scale_ref[0]) + shift_ref[0]
    z = jnp.dot(h.astype(BF16), w_ref[...], preferred_element_type=F32) + b_ref[...]
    d = u_ref.shape[-1]
    u_ref[...] = z[:, :d] * jax.nn.sigmoid(z[:, d:])


def _conv_in(x2d, g, shift, scale, w_in, b_in, *, seq, tm):
    t, d = x2d.shape
    tpb = seq // tm
    bvec = pl.BlockSpec((1, 1, d), lambda i: (i // tpb, 0, 0))
    return pl.pallas_call(
        _conv_in_kernel,
        grid=(t // tm,),
        in_specs=[
            pl.BlockSpec((tm, d), lambda i: (i, 0)),
            _const_spec((1, d)),
            bvec, bvec,
            _const_spec((d, 2 * d)),
            _const_spec((1, 2 * d)),
        ],
        out_specs=pl.BlockSpec((tm, d), lambda i: (i, 0)),
        out_shape=jax.ShapeDtypeStruct((t, d), F32),
        compiler_params=pltpu.CompilerParams(
            dimension_semantics=("arbitrary",), vmem_limit_bytes=V7X_VMEM_LIMIT_BYTES),
        name="conv_in_glu",
    )(x2d, g, shift, scale, w_in, b_in)


def _dwconv_kernel(u_ref, halo_ref, x_ref, wdw_ref, bdw_ref, lng_ref, lnb_ref,
                   wout_ref, bout_ref, gate_ref, o_ref, ext_ref, conv_ref,
                   *, tiles_per_batch, width, row_chunk, col_chunk):
    tm, d = u_ref.shape
    i = pl.program_id(0)
    first = (i % tiles_per_batch) == 0
    ext_ref[0:CONV_HALO, :] = jnp.where(first, 0.0, halo_ref[...])
    ext_ref[CONV_HALO:, :] = u_ref[...]

    off0 = CONV_HALO - (width - 1)

    def chunk(c, _):
        r0 = pl.multiple_of(c * row_chunk, row_chunk)
        for c0 in range(0, d, col_chunk):
            cols = slice(c0, c0 + col_chunk)
            out = jnp.zeros((row_chunk, col_chunk), F32)
            for s in range(8):
                rows = row_chunk + (8 if s else 0)
                part = None
                for k in range(width):
                    if (off0 + k) % 8 != s:
                        continue
                    q = (off0 + k) // 8
                    blk = ext_ref[pl.ds(r0 + 8 * q, rows), cols]
                    term = blk * wdw_ref[k:k + 1, cols]
                    part = term if part is None else part + term
                if part is None:
                    continue
                if s:
                    part = pltpu.roll(part, rows - s, axis=0)[:row_chunk]
                out = out + part
            conv_ref[pl.ds(r0, row_chunk), cols] = out
        return 0

    lax.fori_loop(0, tm // row_chunk, chunk, 0)

    v = conv_ref[...] + bdw_ref[...]
    mu = jnp.mean(v, axis=-1, keepdims=True)
    vc = v - mu
    var = jnp.mean(vc * vc, axis=-1, keepdims=True)
    y = vc * lax.rsqrt(var + EPS) * lng_ref[...] + lnb_ref[...]
    y = y * jax.nn.sigmoid(y)
    z = jnp.dot(y.astype(BF16), wout_ref[...], preferred_element_type=F32) + bout_ref[...]
    o_ref[...] = x_ref[...] + gate_ref[0] * z


def _dwconv(u, x2d, w_dw, b_dw, ln_g, ln_b, w_out, b_out, gate, *, seq, tm):
    t, d = u.shape
    width = w_dw.shape[0]
    assert width - 1 <= CONV_HALO and tm % CONV_HALO == 0
    tpb = seq // tm
    halo_blocks = tm // CONV_HALO
    kern = functools.partial(_dwconv_kernel, tiles_per_batch=tpb, width=width,
                             row_chunk=64, col_chunk=256)
    return pl.pallas_call(
        kern,
        grid=(t // tm,),
        in_specs=[
            pl.BlockSpec((tm, d), lambda i: (i, 0)),
            pl.BlockSpec((CONV_HALO, d), lambda i: (jnp.maximum(i * halo_blocks - 1, 0), 0)),
            pl.BlockSpec((tm, d), lambda i: (i, 0)),
            _const_spec((width, d)),
            _const_spec((1, d)), _const_spec((1, d)), _const_spec((1, d)),
            _const_spec((d, d)),
            _const_spec((1, d)),
            pl.BlockSpec((1, 1, d), lambda i: (i // tpb, 0, 0)),
        ],
        out_specs=pl.BlockSpec((tm, d), lambda i: (i, 0)),
        out_shape=jax.ShapeDtypeStruct((t, d), F32),
        scratch_shapes=[pltpu.VMEM((tm + CONV_HALO, d), F32), pltpu.VMEM((tm, d), F32)],
        compiler_params=pltpu.CompilerParams(
            dimension_semantics=("arbitrary",), vmem_limit_bytes=V7X_VMEM_LIMIT_BYTES),
        name="dwconv_ln_out",
    )(u, u, x2d, w_dw, b_dw, ln_g, ln_b, w_out, b_out, gate)


def _ffn_kernel(*refs, has_attn, has_final):
    refs = list(refs)
    x_ref = refs.pop(0)
    if has_attn:
        o_ref_in, wo_ref, gate_m_ref = refs.pop(0), refs.pop(0), refs.pop(0)
    g_ref, shift_ref, scale_ref, gate_ref, win_ref, wout_ref = (refs.pop(0) for _ in range(6))
    if has_final:
        gf_ref = refs.pop(0)
    out_ref = refs.pop(0)

    x = x_ref[...]
    if has_attn:
        x = x + gate_m_ref[0] * jnp.dot(o_ref_in[...], wo_ref[...], preferred_element_type=F32)
    h = ((_rms(x) * g_ref[...]) * (1.0 + scale_ref[0]) + shift_ref[0]).astype(BF16)
    f = wout_ref.shape[0]
    ug = jnp.dot(h, win_ref[:, :f], preferred_element_type=F32)
    uu = jnp.dot(h, win_ref[:, f:], preferred_element_type=F32)
    a = (ug * jax.nn.sigmoid(ug) * uu).astype(BF16)
    x = x + gate_ref[0] * jnp.dot(a, wout_ref[...], preferred_element_type=F32)
    if has_final:
        x = _rms(x) * gf_ref[...]
    out_ref[...] = x


def _ffn(x2d, g, shift, scale, gate, w_in, w_out, *, seq, tm, attn=None, final_g=None):
    t, d = x2d.shape
    f = w_out.shape[0]
    tpb = seq // tm
    row = pl.BlockSpec((tm, d), lambda i: (i, 0))
    bvec = pl.BlockSpec((1, 1, d), lambda i: (i // tpb, 0, 0))
    args, specs = [x2d], [row]
    if attn is not None:
        o, w_o, gate_m = attn
        args += [o, w_o, gate_m]
        specs += [row, _const_spec(w_o.shape), bvec]
    args += [g, shift, scale, gate, w_in, w_out]
    specs += [_const_spec((1, d)), bvec, bvec, bvec, _const_spec((d, 2 * f)), _const_spec((f, d))]
    if final_g is not None:
        args.append(final_g)
        specs.append(_const_spec((1, d)))
    kern = functools.partial(_ffn_kernel, has_attn=attn is not None, has_final=final_g is not None)
    return pl.pallas_call(
        kern,
        grid=(t // tm,),
        in_specs=specs,
        out_specs=row,
        out_shape=jax.ShapeDtypeStruct((t, d), F32),
        compiler_params=pltpu.CompilerParams(
            dimension_semantics=("arbitrary",), vmem_limit_bytes=V7X_VMEM_LIMIT_BYTES),
        name="swiglu_ffn",
    )(*args)


def _split3(x):
    hi = x.astype(BF16)
    r1 = x - hi.astype(F32)
    mid = r1.astype(BF16)
    lo = (r1 - mid.astype(F32)).astype(BF16)
    return hi, mid, lo


def _kvq_kernel(x_ref, gkv_ref, shkv_ref, sckv_ref, gq_ref, shq_ref, scq_ref,
                wk_ref, wvT_ref, wqT_ref, wf_ref, fb_ref,
                k_ref, kb_ref, vT_ref, qT_ref, cum_ref, carry_ref,
                *, tiles_per_batch, n_heads, q_scale):
    tm = x_ref.shape[0]
    i = pl.program_id(0)

    @pl.when((i % tiles_per_batch) == 0)
    def _():
        carry_ref[...] = jnp.zeros_like(carry_ref)

    r = _rms(x_ref[...])
    hk = ((r * gkv_ref[...]) * (1.0 + sckv_ref[0]) + shkv_ref[0]).astype(BF16)
    hq = ((r * gq_ref[...]) * (1.0 + scq_ref[0]) + shq_ref[0]).astype(BF16)

    k_ref[...] = jnp.dot(hk, wk_ref[...], preferred_element_type=F32).astype(BF16)
    vT_ref[0] = lax.dot_general(wvT_ref[...], hk, _NT, preferred_element_type=F32).astype(BF16)
    qT = lax.dot_general(wqT_ref[...], hq, _NT, preferred_element_type=F32)
    qT_ref[0] = (qT * q_scale).astype(BF16)

    fl = jnp.dot(hk, wf_ref[...], preferred_element_type=F32) + fb_ref[...]
    ls = jnp.minimum(fl, 0.0) - jnp.log1p(jnp.exp(-jnp.abs(fl)))
    row = lax.broadcasted_iota(jnp.int32, ls.shape, 0)
    sh = 1
    while sh < tm:
        ls = ls + jnp.where(row >= sh, pltpu.roll(ls, sh, axis=0), 0.0)
        sh *= 2
    cum = ls + carry_ref[...]
    carry_ref[...] = cum[tm - 1:tm, :]
    cum_ref[...] = cum[:, :n_heads]

    hi, mid, lo = (p.astype(F32) for p in _split3(cum))
    lane = lax.broadcasted_iota(jnp.int32, cum.shape, 1)
    kb = jnp.where(lane < n_heads, -hi,
         jnp.where(lane < 2 * n_heads, -mid,
         jnp.where(lane < 3 * n_heads, -lo,
         jnp.where(lane < 3 * n_heads + 3, 1.0, 0.0))))
    kb_ref[...] = kb.astype(BF16)


def _kvq(x2d, gkv, shkv, sckv, gq, shq, scq, w_k, w_vT, w_qT, w_f, fb, *, batch, seq, tm,
         n_heads, q_scale):
    t, d = x2d.shape
    aw = w_k.shape[1]
    tpb = seq // tm
    row = lambda n: pl.BlockSpec((tm, n), lambda i: (i, 0))
    bvec = pl.BlockSpec((1, 1, d), lambda i: (i // tpb, 0, 0))
    colT = pl.BlockSpec((1, aw, tm), lambda i: (i // tpb, 0, i % tpb))
    kern = functools.partial(_kvq_kernel, tiles_per_batch=tpb, n_heads=n_heads, q_scale=q_scale)
    return pl.pallas_call(
        kern,
        grid=(t // tm,),
        in_specs=[
            row(d),
            _const_spec((1, d)), bvec, bvec,
            _const_spec((1, d)), bvec, bvec,
            _const_spec((d, aw)), _const_spec((aw, d)), _const_spec((aw, d)),
            _const_spec((d, LANES)), _const_spec((1, LANES)),
        ],
        out_specs=[row(aw), row(LANES), colT, colT, row(n_heads)],
        out_shape=[
            jax.ShapeDtypeStruct((t, aw), BF16),
            jax.ShapeDtypeStruct((t, LANES), BF16),
            jax.ShapeDtypeStruct((batch, aw, seq), BF16),
            jax.ShapeDtypeStruct((batch, aw, seq), BF16),
            jax.ShapeDtypeStruct((t, n_heads), F32),
        ],
        scratch_shapes=[pltpu.VMEM((1, LANES), F32)],
        compiler_params=pltpu.CompilerParams(
            dimension_semantics=("arbitrary",), vmem_limit_bytes=V7X_VMEM_LIMIT_BYTES),
        name="kvq_proj",
    )(x2d, gkv, shkv, sckv, gq, shq, scq, w_k, w_vT, w_qT, w_f, fb)


def _attn_kernel(qT_ref, cq_ref, k_ref, kb_ref, vT_ref, o_ref, *, tile, head_dim, n_heads):
    tq = tk = tile
    hd = head_dim
    j = pl.program_id(1)
    qi = pl.program_id(2)

    row128 = lax.broadcasted_iota(jnp.int32, (LANES, tq), 0)
    srow = lax.broadcasted_iota(jnp.int32, (tk, tq), 0)
    tcol = lax.broadcasted_iota(jnp.int32, (tk, tq), 1)
    causal = srow <= tcol

    outs = []
    for hh in range(2):
        h = 2 * j + hh
        qT = qT_ref[0, hh * hd:(hh + 1) * hd, :]
        zq = jnp.zeros_like(qT)
        top = jnp.concatenate([qT, zq] if hh == 0 else [zq, qT], axis=0)
        hi, mid, lo = _split3(cq_ref[0, 0, hh:hh + 1, :])
        pick = (row128 == h) | (row128 == n_heads + h) | (row128 == 2 * n_heads + h)
        bot = jnp.where(pick, 1.0, 0.0)
        bot = jnp.where(row128 == 3 * n_heads, hi.astype(F32), bot)
        bot = jnp.where(row128 == 3 * n_heads + 1, mid.astype(F32), bot)
        bot = jnp.where(row128 == 3 * n_heads + 2, lo.astype(F32), bot)
        q_aug = jnp.concatenate([top, bot.astype(BF16)], axis=0)

        def step(ki, carry, masked, hh=hh, q_aug=q_aug):
            m, l, acc = carry
            ks = pl.multiple_of(ki * tk, tk)
            k_aug = jnp.concatenate(
                [k_ref[0, pl.ds(ks, tk), :], kb_ref[0, pl.ds(ks, tk), :]], axis=1)
            s = jnp.dot(k_aug, q_aug, preferred_element_type=F32)
            if masked:
                s = jnp.where(causal, s, -jnp.inf)
            m_new = jnp.maximum(m, jnp.max(s, axis=0, keepdims=True))
            alpha = jnp.exp(m - m_new)
            p = jnp.exp(s - m_new)
            l = alpha * l + jnp.sum(p, axis=0, keepdims=True)
            v = vT_ref[0, hh * hd:(hh + 1) * hd, pl.ds(ks, tk)]
            acc = alpha * acc + jnp.dot(v, p.astype(BF16), preferred_element_type=F32)
            return m_new, l, acc

        init = (jnp.full((1, tq), -jnp.inf, F32), jnp.zeros((1, tq), F32),
                jnp.zeros((hd, tq), F32))
        carry = lax.fori_loop(0, qi, functools.partial(step, masked=False), init)
        m, l, acc = step(qi, carry, True)
        outs.append(acc / l)

    o_ref[0] = jnp.concatenate(outs, axis=0).T.astype(BF16)


def _attention(qT, cqT, k, kb, vT, *, tile, head_dim, n_heads):
    b, aw, s = qT.shape
    pair = 2 * head_dim
    assert pair == LANES
    kern = functools.partial(_attn_kernel, tile=tile, head_dim=head_dim, n_heads=n_heads)
    return pl.pallas_call(
        kern,
        grid=(b, n_heads // 2, s // tile),
        in_specs=[
            pl.BlockSpec((1, pair, tile), lambda bi, j, qi: (bi, j, qi)),
            pl.BlockSpec((1, 1, 2, tile), lambda bi, j, qi: (bi, j, 0, qi)),
            pl.BlockSpec((1, s, pair), lambda bi, j, qi: (bi, 0, j)),
            pl.BlockSpec((1, s, LANES), lambda bi, j, qi: (bi, 0, 0)),
            pl.BlockSpec((1, pair, s), lambda bi, j, qi: (bi, j, 0)),
        ],
        out_specs=pl.BlockSpec((1, tile, pair), lambda bi, j, qi: (bi, qi, j)),
        out_shape=jax.ShapeDtypeStruct((b, s, aw), BF16),
        compiler_params=pltpu.CompilerParams(
            dimension_semantics=("arbitrary", "arbitrary", "arbitrary"),
            vmem_limit_bytes=V7X_VMEM_LIMIT_BYTES),
        name="forgetting_attention",
    )(qT, cqT, k, kb, vT)


def kernel(x, c, mix_norm_g, mix_ada_w, mix_ada_b, ffn_norm_g, ffn_ada_w, ffn_ada_b, ffn_w_in, ffn_w_out, conv_w_in, conv_b_in, conv_w_dw, conv_b_dw, conv_ln_g, conv_ln_b, conv_w_out, conv_b_out, kv_norm_g, kv_ada_w, kv_ada_b, kv_w, forget_b, attn_w_q, attn_w_o, final_norm_g):
    batch, seq, d = x.shape
    depth = mix_norm_g.shape[0]
    n_a = conv_w_in.shape[0]
    n_heads = forget_b.shape[0]
    aw = attn_w_q.shape[-1]
    head_dim = aw // n_heads
    assert depth == 2 and n_a == 1 and attn_w_q.shape[0] == 1
    assert 3 * n_heads + 3 <= LANES
    t = batch * seq
    tm = 512

    c_pad = jnp.zeros((8, d), F32).at[:batch].set(c)
    mix_ada = _ada(c_pad, mix_ada_w, mix_ada_b)[:, :batch]
    ffn_ada = _ada(c_pad, ffn_ada_w, ffn_ada_b)[:, :batch]
    kv_ada = _ada(c_pad, kv_ada_w[None], kv_ada_b[None])[0, :batch]

    def vecs(a, n):
        return [v.reshape(batch, 1, d) for v in jnp.split(a, n, axis=-1)]

    row = lambda v: v.reshape(1, -1)
    x2d = x.reshape(t, d)

    shift, scale, gate = vecs(mix_ada[0], 3)
    u = _conv_in(x2d, row(mix_norm_g[0]), shift, scale, conv_w_in[0].astype(BF16),
                 row(conv_b_in[0]), seq=seq, tm=tm)
    x2d = _dwconv(u, x2d, conv_w_dw[0], row(conv_b_dw[0]), row(conv_ln_g[0]), row(conv_ln_b[0]),
                  conv_w_out[0].astype(BF16), row(conv_b_out[0]), gate, seq=seq, tm=tm)
    shift, scale, gate = vecs(ffn_ada[0], 3)
    x2d = _ffn(x2d, row(ffn_norm_g[0]), shift, scale, gate, ffn_w_in[0].astype(BF16),
               ffn_w_out[0].astype(BF16), seq=seq, tm=tm)

    shkv, sckv = vecs(kv_ada, 2)
    shq, scq, gate_m = vecs(mix_ada[1], 3)
    w_k = kv_w[:, :aw].astype(BF16)
    w_vT = kv_w[:, aw:2 * aw].T.astype(BF16)
    w_qT = attn_w_q[0].T.astype(BF16)
    w_f = kv_w[:, 2 * aw:]
    pad = LANES - 3 * n_heads
    w_f_rep = jnp.concatenate([w_f, w_f, w_f, jnp.zeros((d, pad), F32)], axis=1).astype(BF16)
    fb_rep = jnp.concatenate([forget_b, forget_b, forget_b, jnp.zeros((pad,), F32)]).reshape(1, LANES)
    k, kb, vT, qT, cum = _kvq(x2d, row(kv_norm_g), shkv, sckv, row(mix_norm_g[1]), shq, scq,
                              w_k, w_vT, w_qT, w_f_rep, fb_rep, batch=batch, seq=seq, tm=tm,
                              n_heads=n_heads, q_scale=head_dim ** -0.5)

    cqT = cum.reshape(batch, seq, n_heads // 2, 2).transpose(0, 2, 3, 1)
    o = _attention(qT, cqT, k.reshape(batch, seq, aw), kb.reshape(batch, seq, LANES), vT,
                   tile=512, head_dim=head_dim, n_heads=n_heads)
    shift, scale, gate = vecs(ffn_ada[1], 3)
    out = _ffn(x2d, row(ffn_norm_g[1]), shift, scale, gate, ffn_w_in[1].astype(BF16),
               ffn_w_out[1].astype(BF16), seq=seq, tm=tm,
               attn=(o.reshape(t, aw), attn_w_o[0].astype(BF16), gate_m),
               final_g=row(final_norm_g))
    return out.reshape(batch, seq, d)
```

```python
import functools

import jax
import jax.numpy as jnp
from jax import lax
from jax.experimental import pallas as pl
from jax.experimental.pallas import tpu as pltpu

F32 = jnp.float32
BF16 = jnp.bfloat16

EPS = 1e-6
V7X_VMEM_LIMIT_BYTES = 56 * 1024 * 1024
LANES = 128
CONV_HALO = 32

LOG2E = 1.4426950408889634

_NT = (((1,), (1,)), ((), ()))


def _const_spec(shape):
    return pl.BlockSpec(shape, lambda *_: (0,) * len(shape), pipeline_mode=pl.Buffered(1))


def _rms(x):
    return x * lax.rsqrt(jnp.mean(x * x, axis=-1, keepdims=True) + EPS)


def _ada_kernel(c_ref, w_ref, b_ref, o_ref):
    c = c_ref[...]
    ca = c * jax.nn.sigmoid(c)
    o_ref[...] = jnp.dot(ca, w_ref[...], precision=lax.Precision.HIGHEST,
                         preferred_element_type=F32) + b_ref[...]


def _ada(c_pad, w, b, tn=512):
    n_l, d, n = w.shape
    rows = c_pad.shape[0]
    return pl.pallas_call(
        _ada_kernel,
        grid=(n_l, n // tn),
        in_specs=[
            pl.BlockSpec((rows, d), lambda l, j: (0, 0)),
            pl.BlockSpec((None, d, tn), lambda l, j: (l, 0, j)),
            pl.BlockSpec((None, 1, tn), lambda l, j: (l, 0, j)),
        ],
        out_specs=pl.BlockSpec((None, rows, tn), lambda l, j: (l, 0, j)),
        out_shape=jax.ShapeDtypeStruct((n_l, rows, n), F32),
        name="ada_proj",
    )(c_pad, w, b.reshape(n_l, 1, n))


def _conv_in_kernel(x_ref, g_ref, shift_ref, scale_ref, w_ref, b_ref, u_ref):
    h = (_rms(x_ref[...]) * g_ref[...]) * (1.0 + scale_ref[0]) + shift_ref[0]
    z = jnp.dot(h.astype(BF16), w_ref[...], preferred_element_type=F32) + b_ref[...]
    d = u_ref.shape[-1]
    u_ref[...] = z[:, :d] * jax.nn.sigmoid(z[:, d:])


def _conv_in(x2d, g, shift, scale, w_in, b_in, *, seq, tm):
    t, d = x2d.shape
    tpb = seq // tm
    bvec = pl.BlockSpec((1, 1, d), lambda i: (i // tpb, 0, 0))
    return pl.pallas_call(
        _conv_in_kernel,
        grid=(t // tm,),
        in_specs=[
            pl.BlockSpec((tm, d), lambda i: (i, 0)),
            _const_spec((1, d)),
            bvec, bvec,
            _const_spec((d, 2 * d)),
            _const_spec((1, 2 * d)),
        ],
        out_specs=pl.BlockSpec((tm, d), lambda i: (i, 0)),
        out_shape=jax.ShapeDtypeStruct((t, d), F32),
        compiler_params=pltpu.CompilerParams(
            dimension_semantics=("arbitrary",), vmem_limit_bytes=V7X_VMEM_LIMIT_BYTES),
        name="conv_in_glu",
    )(x2d, g, shift, scale, w_in, b_in)


def _dwconv_kernel(u_ref, halo_ref, x_ref, wdw_ref, bdw_ref, lng_ref, lnb_ref,
                   wout_ref, bout_ref, gate_ref, o_ref, ext_ref, conv_ref,
                   *, tiles_per_batch, width, row_chunk, col_chunk):
    tm, d = u_ref.shape
    i = pl.program_id(0)
    first = (i % tiles_per_batch) == 0
    ext_ref[0:CONV_HALO, :] = jnp.where(first, 0.0, halo_ref[...])
    ext_ref[CONV_HALO:, :] = u_ref[...]

    off0 = CONV_HALO - (width - 1)

    def chunk(c, _):
        r0 = pl.multiple_of(c * row_chunk, row_chunk)
        for c0 in range(0, d, col_chunk):
            cols = slice(c0, c0 + col_chunk)
            out = jnp.zeros((row_chunk, col_chunk), F32)
            for s in range(8):
                rows = row_chunk + (8 if s else 0)
                part = None
                for k in range(width):
                    if (off0 + k) % 8 != s:
                        continue
                    q = (off0 + k) // 8
                    blk = ext_ref[pl.ds(r0 + 8 * q, rows), cols]
                    term = blk * wdw_ref[k:k + 1, cols]
                    part = term if part is None else part + term
                if part is None:
                    continue
                if s:
                    part = pltpu.roll(part, rows - s, axis=0)[:row_chunk]
                out = out + part
            conv_ref[pl.ds(r0, row_chunk), cols] = out
        return 0

    lax.fori_loop(0, tm // row_chunk, chunk, 0)

    v = conv_ref[...] + bdw_ref[...]
    mu = jnp.mean(v, axis=-1, keepdims=True)
    vc = v - mu
    var = jnp.mean(vc * vc, axis=-1, keepdims=True)
    y = vc * lax.rsqrt(var + EPS) * lng_ref[...] + lnb_ref[...]
    y = y * jax.nn.sigmoid(y)
    z = jnp.dot(y.astype(BF16), wout_ref[...], preferred_element_type=F32) + bout_ref[...]
    o_ref[...] = x_ref[...] + gate_ref[0] * z


def _dwconv(u, x2d, w_dw, b_dw, ln_g, ln_b, w_out, b_out, gate, *, seq, tm):
    t, d = u.shape
    width = w_dw.shape[0]
    assert width - 1 <= CONV_HALO and tm % CONV_HALO == 0
    tpb = seq // tm
    halo_blocks = tm // CONV_HALO
    kern = functools.partial(_dwconv_kernel, tiles_per_batch=tpb, width=width,
                             row_chunk=64, col_chunk=256)
    return pl.pallas_call(
        kern,
        grid=(t // tm,),
        in_specs=[
            pl.BlockSpec((tm, d), lambda i: (i, 0)),
            pl.BlockSpec((CONV_HALO, d), lambda i: (jnp.maximum(i * halo_blocks - 1, 0), 0)),
            pl.BlockSpec((tm, d), lambda i: (i, 0)),
            _const_spec((width, d)),
            _const_spec((1, d)), _const_spec((1, d)), _const_spec((1, d)),
            _const_spec((d, d)),
            _const_spec((1, d)),
            pl.BlockSpec((1, 1, d), lambda i: (i // tpb, 0, 0)),
        ],
        out_specs=pl.BlockSpec((tm, d), lambda i: (i, 0)),
        out_shape=jax.ShapeDtypeStruct((t, d), F32),
        scratch_shapes=[pltpu.VMEM((tm + CONV_HALO, d), F32), pltpu.VMEM((tm, d), F32)],
        compiler_params=pltpu.CompilerParams(
            dimension_semantics=("arbitrary",), vmem_limit_bytes=V7X_VMEM_LIMIT_BYTES),
        name="dwconv_ln_out",
    )(u, u, x2d, w_dw, b_dw, ln_g, ln_b, w_out, b_out, gate)


def _ffn_kernel(*refs, has_attn, has_final):
    refs = list(refs)
    x_ref = refs.pop(0)
    if has_attn:
        o_ref_in, wo_ref, gate_m_ref = refs.pop(0), refs.pop(0), refs.pop(0)
    g_ref, shift_ref, scale_ref, gate_ref, win_ref, wout_ref = (refs.pop(0) for _ in range(6))
    if has_final:
        gf_ref = refs.pop(0)
    out_ref = refs.pop(0)

    x = x_ref[...]
    if has_attn:
        x = x + gate_m_ref[0] * jnp.dot(o_ref_in[...], wo_ref[...], preferred_element_type=F32)
    h = ((_rms(x) * g_ref[...]) * (1.0 + scale_ref[0]) + shift_ref[0]).astype(BF16)
    f = wout_ref.shape[0]
    ug = jnp.dot(h, win_ref[:, :f], preferred_element_type=F32)
    uu = jnp.dot(h, win_ref[:, f:], preferred_element_type=F32)
    a = (ug * jax.nn.sigmoid(ug) * uu).astype(BF16)
    x = x + gate_ref[0] * jnp.dot(a, wout_ref[...], preferred_element_type=F32)
    if has_final:
        x = _rms(x) * gf_ref[...]
    out_ref[...] = x


def _ffn(x2d, g, shift, scale, gate, w_in, w_out, *, seq, tm, attn=None, final_g=None):
    t, d = x2d.shape
    f = w_out.shape[0]
    tpb = seq // tm
    row = pl.BlockSpec((tm, d), lambda i: (i, 0))
    bvec = pl.BlockSpec((1, 1, d), lambda i: (i // tpb, 0, 0))
    args, specs = [x2d], [row]
    if attn is not None:
        o, w_o, gate_m = attn
        args += [o, w_o, gate_m]
        specs += [row, _const_spec(w_o.shape), bvec]
    args += [g, shift, scale, gate, w_in, w_out]
    specs += [_const_spec((1, d)), bvec, bvec, bvec, _const_spec((d, 2 * f)), _const_spec((f, d))]
    if final_g is not None:
        args.append(final_g)
        specs.append(_const_spec((1, d)))
    kern = functools.partial(_ffn_kernel, has_attn=attn is not None, has_final=final_g is not None)
    return pl.pallas_call(
        kern,
        grid=(t // tm,),
        in_specs=specs,
        out_specs=row,
        out_shape=jax.ShapeDtypeStruct((t, d), F32),
        compiler_params=pltpu.CompilerParams(
            dimension_semantics=("arbitrary",), vmem_limit_bytes=V7X_VMEM_LIMIT_BYTES),
        name="swiglu_ffn",
    )(*args)


def _split3(x):
    hi = x.astype(BF16)
    r1 = x - hi.astype(F32)
    mid = r1.astype(BF16)
    lo = (r1 - mid.astype(F32)).astype(BF16)
    return hi, mid, lo


def _kvq_kernel(x_ref, gkv_ref, shkv_ref, sckv_ref, gq_ref, shq_ref, scq_ref,
                wk_ref, wvT_ref, wqT_ref, wf_ref, fb_ref,
                k_ref, kb_ref, vT_ref, qT_ref, cum_ref, carry_ref,
                *, tiles_per_batch, n_heads, q_scale):
    tm = x_ref.shape[0]
    i = pl.program_id(0)

    @pl.when((i % tiles_per_batch) == 0)
    def _():
        carry_ref[...] = jnp.zeros_like(carry_ref)

    r = _rms(x_ref[...])
    hk = ((r * gkv_ref[...]) * (1.0 + sckv_ref[0]) + shkv_ref[0]).astype(BF16)
    hq = ((r * gq_ref[...]) * (1.0 + scq_ref[0]) + shq_ref[0]).astype(BF16)

    k_ref[...] = jnp.dot(hk, wk_ref[...], preferred_element_type=F32).astype(BF16)
    vT_ref[0] = lax.dot_general(wvT_ref[...], hk, _NT, preferred_element_type=F32).astype(BF16)
    qT = lax.dot_general(wqT_ref[...], hq, _NT, preferred_element_type=F32)
    qT_ref[0] = (qT * q_scale).astype(BF16)

    fl = jnp.dot(hk, wf_ref[...], preferred_element_type=F32) + fb_ref[...]
    ls = jnp.minimum(fl, 0.0) - jnp.log1p(jnp.exp(-jnp.abs(fl)))
    row = lax.broadcasted_iota(jnp.int32, ls.shape, 0)
    sh = 1
    while sh < tm:
        ls = ls + jnp.where(row >= sh, pltpu.roll(ls, sh, axis=0), 0.0)
        sh *= 2
    cum = ls + carry_ref[...]
    carry_ref[...] = cum[tm - 1:tm, :]
    cum2 = cum * LOG2E
    cum_ref[...] = cum2[:, :n_heads]

    hi, mid, lo = (p.astype(F32) for p in _split3(cum2))
    lane = lax.broadcasted_iota(jnp.int32, cum.shape, 1)
    kb = jnp.where(lane < n_heads, -hi,
         jnp.where(lane < 2 * n_heads, -mid,
         jnp.where(lane < 3 * n_heads, -lo,
         jnp.where(lane < 3 * n_heads + 3, 1.0, 0.0))))
    kb_ref[...] = kb.astype(BF16)


def _kvq(x2d, gkv, shkv, sckv, gq, shq, scq, w_k, w_vT, w_qT, w_f, fb, *, batch, seq, tm,
         n_heads, q_scale):
    t, d = x2d.shape
    aw = w_k.shape[1]
    tpb = seq // tm
    row = lambda n: pl.BlockSpec((tm, n), lambda i: (i, 0))
    bvec = pl.BlockSpec((1, 1, d), lambda i: (i // tpb, 0, 0))
    colT = pl.BlockSpec((1, aw, tm), lambda i: (i // tpb, 0, i % tpb))
    kern = functools.partial(_kvq_kernel, tiles_per_batch=tpb, n_heads=n_heads, q_scale=q_scale)
    return pl.pallas_call(
        kern,
        grid=(t // tm,),
        in_specs=[
            row(d),
            _const_spec((1, d)), bvec, bvec,
            _const_spec((1, d)), bvec, bvec,
            _const_spec((d, aw)), _const_spec((aw, d)), _const_spec((aw, d)),
            _const_spec((d, LANES)), _const_spec((1, LANES)),
        ],
        out_specs=[row(aw), row(LANES), colT, colT, row(n_heads)],
        out_shape=[
            jax.ShapeDtypeStruct((t, aw), BF16),
            jax.ShapeDtypeStruct((t, LANES), BF16),
            jax.ShapeDtypeStruct((batch, aw, seq), BF16),
            jax.ShapeDtypeStruct((batch, aw, seq), BF16),
            jax.ShapeDtypeStruct((t, n_heads), F32),
        ],
        scratch_shapes=[pltpu.VMEM((1, LANES), F32)],
        compiler_params=pltpu.CompilerParams(
            dimension_semantics=("arbitrary",), vmem_limit_bytes=V7X_VMEM_LIMIT_BYTES),
        name="kvq_proj",
    )(x2d, gkv, shkv, sckv, gq, shq, scq, w_k, w_vT, w_qT, w_f, fb)


def _attn_kernel(qT_ref, cq_ref, k_ref, kb_ref, vT_ref, o_ref, s_ref, *, tile, head_dim, n_heads):
    tq = tk = tile
    hd = head_dim
    half = tile // 2
    j = pl.program_id(1)
    qi = pl.program_id(2)

    row128 = lax.broadcasted_iota(jnp.int32, (LANES, tq), 0)
    causal = (lax.broadcasted_iota(jnp.int32, (half, half), 0)
              <= lax.broadcasted_iota(jnp.int32, (half, half), 1))

    q_augs = []
    for hh in range(2):
        h = 2 * j + hh
        qT = qT_ref[0, hh * hd:(hh + 1) * hd, :]
        zq = jnp.zeros_like(qT)
        top = jnp.concatenate([qT, zq] if hh == 0 else [zq, qT], axis=0)
        hi, mid, lo = _split3(cq_ref[0, 0, hh:hh + 1, :])
        pick = (row128 == h) | (row128 == n_heads + h) | (row128 == 2 * n_heads + h)
        bot = jnp.where(pick, 1.0, 0.0)
        bot = jnp.where(row128 == 3 * n_heads, hi.astype(F32), bot)
        bot = jnp.where(row128 == 3 * n_heads + 1, mid.astype(F32), bot)
        bot = jnp.where(row128 == 3 * n_heads + 2, lo.astype(F32), bot)
        q_augs.append(jnp.concatenate([top, bot.astype(BF16)], axis=0))

    def produce(slot, tile_idx):
        ks = pl.multiple_of(tile_idx * tk, tk)
        k_aug = jnp.concatenate(
            [k_ref[0, pl.ds(ks, tk), :], kb_ref[0, pl.ds(ks, tk), :]], axis=1)
        maxes = []
        for hh in range(2):
            s = jnp.dot(k_aug, q_augs[hh], preferred_element_type=F32)
            s_ref[slot, hh] = s
            maxes.append(jnp.max(s, axis=0, keepdims=True))
        return tuple(maxes)

    def consume(state, hh, s, smax, k_start):
        m, l, acc = state
        m_new = jnp.maximum(m, smax)
        alpha = jnp.exp2(m - m_new)
        p = jnp.exp2(s - m_new)
        l = alpha * l + jnp.sum(p, axis=0, keepdims=True)
        v = vT_ref[0, hh * hd:(hh + 1) * hd, pl.ds(k_start, s.shape[0])]
        acc = alpha * acc + jnp.dot(v, p.astype(BF16), preferred_element_type=F32)
        return m_new, l, acc

    def step(carry, slot_in, slot_out, tile_idx):
        states, maxes = carry
        new_maxes = produce(slot_out, tile_idx + 1)
        ks = pl.multiple_of(tile_idx * tk, tk)
        states = tuple(consume(states[hh], hh, s_ref[slot_in, hh], maxes[hh], ks)
                       for hh in range(2))
        return states, new_maxes

    def two_steps(kk, carry):
        carry = step(carry, 0, 1, 2 * kk)
        return step(carry, 1, 0, 2 * kk + 1)

    init = tuple((jnp.full((1, tq), -jnp.inf, F32), jnp.zeros((1, tq), F32),
                  jnp.zeros((hd, tq), F32)) for _ in range(2))
    carry = (init, produce(0, 0))
    carry = lax.fori_loop(0, qi // 2, two_steps, carry)
    carry = lax.fori_loop(0, qi % 2, lambda _, c: step(c, 0, 1, qi - 1), carry)
    states, _ = carry

    slot = qi % 2
    d0 = qi * tk
    outs = []
    for hh in range(2):
        m, l, acc = states[hh]
        lo_q = (m[:, :half], l[:, :half], acc[:, :half])
        hi_q = (m[:, half:], l[:, half:], acc[:, half:])
        s_a = jnp.where(causal, s_ref[slot, hh, 0:half, 0:half], -jnp.inf)
        lo_q = consume(lo_q, hh, s_a, jnp.max(s_a, axis=0, keepdims=True),
                       pl.multiple_of(d0, half))
        s_b = s_ref[slot, hh, 0:half, half:tq]
        hi_q = consume(hi_q, hh, s_b, jnp.max(s_b, axis=0, keepdims=True),
                       pl.multiple_of(d0, half))
        s_c = jnp.where(causal, s_ref[slot, hh, half:tk, half:tq], -jnp.inf)
        hi_q = consume(hi_q, hh, s_c, jnp.max(s_c, axis=0, keepdims=True),
                       pl.multiple_of(d0 + half, half))
        outs.append(jnp.concatenate([lo_q[2] / lo_q[1], hi_q[2] / hi_q[1]], axis=1))
    o_ref[0] = jnp.concatenate(outs, axis=0).T.astype(BF16)


def _attention(qT, cqT, k, kb, vT, *, tile, head_dim, n_heads):
    b, aw, s = qT.shape
    pair = 2 * head_dim
    assert pair == LANES
    kern = functools.partial(_attn_kernel, tile=tile, head_dim=head_dim, n_heads=n_heads)
    return pl.pallas_call(
        kern,
        grid=(b, n_heads // 2, s // tile),
        in_specs=[
            pl.BlockSpec((1, pair, tile), lambda bi, j, qi: (bi, j, qi)),
            pl.BlockSpec((1, 1, 2, tile), lambda bi, j, qi: (bi, j, 0, qi)),
            pl.BlockSpec((1, s, pair), lambda bi, j, qi: (bi, 0, j)),
            pl.BlockSpec((1, s, LANES), lambda bi, j, qi: (bi, 0, 0)),
            pl.BlockSpec((1, pair, s), lambda bi, j, qi: (bi, j, 0)),
        ],
        out_specs=pl.BlockSpec((1, tile, pair), lambda bi, j, qi: (bi, qi, j)),
        out_shape=jax.ShapeDtypeStruct((b, s, aw), BF16),
        scratch_shapes=[pltpu.VMEM((2, 2, tile, tile), F32)],
        compiler_params=pltpu.CompilerParams(
            dimension_semantics=("arbitrary", "arbitrary", "arbitrary"),
            vmem_limit_bytes=V7X_VMEM_LIMIT_BYTES),
        name="forgetting_attention",
    )(qT, cqT, k, kb, vT)


def kernel(x, c, mix_norm_g, mix_ada_w, mix_ada_b, ffn_norm_g, ffn_ada_w, ffn_ada_b, ffn_w_in, ffn_w_out, conv_w_in, conv_b_in, conv_w_dw, conv_b_dw, conv_ln_g, conv_ln_b, conv_w_out, conv_b_out, kv_norm_g, kv_ada_w, kv_ada_b, kv_w, forget_b, attn_w_q, attn_w_o, final_norm_g):
    batch, seq, d = x.shape
    depth = mix_norm_g.shape[0]
    n_a = conv_w_in.shape[0]
    n_heads = forget_b.shape[0]
    aw = attn_w_q.shape[-1]
    head_dim = aw // n_heads
    assert depth == 2 and n_a == 1 and attn_w_q.shape[0] == 1
    assert 3 * n_heads + 3 <= LANES
    t = batch * seq
    tm = 512

    c_pad = jnp.zeros((8, d), F32).at[:batch].set(c)
    mix_ada = _ada(c_pad, mix_ada_w, mix_ada_b)[:, :batch]
    ffn_ada = _ada(c_pad, ffn_ada_w, ffn_ada_b)[:, :batch]
    kv_ada = _ada(c_pad, kv_ada_w[None], kv_ada_b[None])[0, :batch]

    def vecs(a, n):
        return [v.reshape(batch, 1, d) for v in jnp.split(a, n, axis=-1)]

    row = lambda v: v.reshape(1, -1)
    x2d = x.reshape(t, d)

    shift, scale, gate = vecs(mix_ada[0], 3)
    u = _conv_in(x2d, row(mix_norm_g[0]), shift, scale, conv_w_in[0].astype(BF16),
                 row(conv_b_in[0]), seq=seq, tm=tm)
    x2d = _dwconv(u, x2d, conv_w_dw[0], row(conv_b_dw[0]), row(conv_ln_g[0]), row(conv_ln_b[0]),
                  conv_w_out[0].astype(BF16), row(conv_b_out[0]), gate, seq=seq, tm=tm)
    shift, scale, gate = vecs(ffn_ada[0], 3)
    x2d = _ffn(x2d, row(ffn_norm_g[0]), shift, scale, gate, ffn_w_in[0].astype(BF16),
               ffn_w_out[0].astype(BF16), seq=seq, tm=tm)

    shkv, sckv = vecs(kv_ada, 2)
    shq, scq, gate_m = vecs(mix_ada[1], 3)
    w_k = kv_w[:, :aw].astype(BF16)
    w_vT = kv_w[:, aw:2 * aw].T.astype(BF16)
    w_qT = attn_w_q[0].T.astype(BF16)
    w_f = kv_w[:, 2 * aw:]
    pad = LANES - 3 * n_heads
    w_f_rep = jnp.concatenate([w_f, w_f, w_f, jnp.zeros((d, pad), F32)], axis=1).astype(BF16)
    fb_rep = jnp.concatenate([forget_b, forget_b, forget_b, jnp.zeros((pad,), F32)]).reshape(1, LANES)
    k, kb, vT, qT, cum = _kvq(x2d, row(kv_norm_g), shkv, sckv, row(mix_norm_g[1]), shq, scq,
                              w_k, w_vT, w_qT, w_f_rep, fb_rep, batch=batch, seq=seq, tm=tm,
                              n_heads=n_heads, q_scale=head_dim ** -0.5 * LOG2E)

    cqT = cum.reshape(batch, seq, n_heads // 2, 2).transpose(0, 2, 3, 1)
    o = _attention(qT, cqT, k.reshape(batch, seq, aw), kb.reshape(batch, seq, LANES), vT,
                   tile=1024, head_dim=head_dim, n_heads=n_heads)
    shift, scale, gate = vecs(ffn_ada[1], 3)
    out = _ffn(x2d, row(ffn_norm_g[1]), shift, scale, gate, ffn_w_in[1].astype(BF16),
               ffn_w_out[1].astype(BF16), seq=seq, tm=tm,
               attn=(o.reshape(t, aw), attn_w_o[0].astype(BF16), gate_m),
               final_g=row(final_norm_g))
    return out.reshape(batch, seq, d)
```

```python
import functools

import jax
import jax.numpy as jnp
from jax import lax
from jax.experimental import pallas as pl
from jax.experimental.pallas import tpu as pltpu

F32 = jnp.float32
BF16 = jnp.bfloat16

EPS = 1e-6
V7X_VMEM_LIMIT_BYTES = 56 * 1024 * 1024
LANES = 128
CONV_HALO = 32

LOG2E = 1.4426950408889634

ROW_TILE = 512
CONV_IN_TILE = 1024
CONV_IN_SUB = 512
ATTN_Q_TILE = 2048
ATTN_K_TILE = 512

_NT = (((1,), (1,)), ((), ()))


def _const_spec(shape):
    return pl.BlockSpec(shape, lambda *_: (0,) * len(shape), pipeline_mode=pl.Buffered(1))


def _rms(x):
    return x * lax.rsqrt(jnp.mean(x * x, axis=-1, keepdims=True) + EPS)


def _ada_kernel(c_ref, w_ref, b_ref, o_ref):
    c = c_ref[...]
    ca = c * jax.nn.sigmoid(c)
    o_ref[...] = jnp.dot(ca, w_ref[...], precision=lax.Precision.HIGHEST,
                         preferred_element_type=F32) + b_ref[...]


def _ada(c_pad, w, b, tn=512):
    n_l, d, n = w.shape
    rows = c_pad.shape[0]
    return pl.pallas_call(
        _ada_kernel,
        grid=(n_l, n // tn),
        in_specs=[
            pl.BlockSpec((rows, d), lambda l, j: (0, 0)),
            pl.BlockSpec((None, d, tn), lambda l, j: (l, 0, j)),
            pl.BlockSpec((None, 1, tn), lambda l, j: (l, 0, j)),
        ],
        out_specs=pl.BlockSpec((None, rows, tn), lambda l, j: (l, 0, j)),
        out_shape=jax.ShapeDtypeStruct((n_l, rows, n), F32),
        name="ada_proj",
    )(c_pad, w, b.reshape(n_l, 1, n))


def _conv_in_kernel(x_ref, g_ref, shift_ref, scale_ref, w_ref, b_ref, u_ref, *, sub):
    tm, d = u_ref.shape
    for r in range(0, tm, sub):
        rows = slice(r, r + sub)
        h = (_rms(x_ref[rows, :]) * g_ref[...]) * (1.0 + scale_ref[0]) + shift_ref[0]
        z = jnp.dot(h.astype(BF16), w_ref[...], preferred_element_type=F32) + b_ref[...]
        u_ref[rows, :] = z[:, :d] * jax.nn.sigmoid(z[:, d:])


def _conv_in(x2d, g, shift, scale, w_in, b_in, *, seq, tm, sub):
    t, d = x2d.shape
    tpb = seq // tm
    bvec = pl.BlockSpec((1, 1, d), lambda i: (i // tpb, 0, 0))
    return pl.pallas_call(
        functools.partial(_conv_in_kernel, sub=sub),
        grid=(t // tm,),
        in_specs=[
            pl.BlockSpec((tm, d), lambda i: (i, 0)),
            _const_spec((1, d)),
            bvec, bvec,
            _const_spec((d, 2 * d)),
            _const_spec((1, 2 * d)),
        ],
        out_specs=pl.BlockSpec((tm, d), lambda i: (i, 0)),
        out_shape=jax.ShapeDtypeStruct((t, d), F32),
        compiler_params=pltpu.CompilerParams(
            dimension_semantics=("arbitrary",), vmem_limit_bytes=V7X_VMEM_LIMIT_BYTES),
        name="conv_in_glu",
    )(x2d, g, shift, scale, w_in, b_in)


def _dwconv_kernel(u_ref, halo_ref, x_ref, wdw_ref, bdw_ref, lng_ref, lnb_ref,
                   wout_ref, bout_ref, gate_ref, o_ref, ext_ref, conv_ref,
                   *, tiles_per_batch, width, row_chunk, col_chunk):
    tm, d = u_ref.shape
    i = pl.program_id(0)
    first = (i % tiles_per_batch) == 0
    ext_ref[0:CONV_HALO, :] = jnp.where(first, 0.0, halo_ref[...])
    ext_ref[CONV_HALO:, :] = u_ref[...]

    off0 = CONV_HALO - (width - 1)

    def chunk(c, _):
        r0 = pl.multiple_of(c * row_chunk, row_chunk)
        for c0 in range(0, d, col_chunk):
            cols = slice(c0, c0 + col_chunk)
            out = jnp.zeros((row_chunk, col_chunk), F32)
            for s in range(8):
                rows = row_chunk + (8 if s else 0)
                part = None
                for k in range(width):
                    if (off0 + k) % 8 != s:
                        continue
                    q = (off0 + k) // 8
                    blk = ext_ref[pl.ds(r0 + 8 * q, rows), cols]
                    term = blk * wdw_ref[k:k + 1, cols]
                    part = term if part is None else part + term
                if part is None:
                    continue
                if s:
                    part = pltpu.roll(part, rows - s, axis=0)[:row_chunk]
                out = out + part
            conv_ref[pl.ds(r0, row_chunk), cols] = out
        return 0

    lax.fori_loop(0, tm // row_chunk, chunk, 0)

    v = conv_ref[...] + bdw_ref[...]
    mu = jnp.mean(v, axis=-1, keepdims=True)
    vc = v - mu
    var = jnp.mean(vc * vc, axis=-1, keepdims=True)
    y = vc * lax.rsqrt(var + EPS) * lng_ref[...] + lnb_ref[...]
    y = y * jax.nn.sigmoid(y)
    z = jnp.dot(y.astype(BF16), wout_ref[...], preferred_element_type=F32) + bout_ref[...]
    o_ref[...] = x_ref[...] + gate_ref[0] * z


def _dwconv(u, x2d, w_dw, b_dw, ln_g, ln_b, w_out, b_out, gate, *, seq, tm):
    t, d = u.shape
    width = w_dw.shape[0]
    assert width - 1 <= CONV_HALO and tm % CONV_HALO == 0
    tpb = seq // tm
    halo_blocks = tm // CONV_HALO
    kern = functools.partial(_dwconv_kernel, tiles_per_batch=tpb, width=width,
                             row_chunk=64, col_chunk=256)
    return pl.pallas_call(
        kern,
        grid=(t // tm,),
        in_specs=[
            pl.BlockSpec((tm, d), lambda i: (i, 0)),
            pl.BlockSpec((CONV_HALO, d), lambda i: (jnp.maximum(i * halo_blocks - 1, 0), 0)),
            pl.BlockSpec((tm, d), lambda i: (i, 0)),
            _const_spec((width, d)),
            _const_spec((1, d)), _const_spec((1, d)), _const_spec((1, d)),
            _const_spec((d, d)),
            _const_spec((1, d)),
            pl.BlockSpec((1, 1, d), lambda i: (i // tpb, 0, 0)),
        ],
        out_specs=pl.BlockSpec((tm, d), lambda i: (i, 0)),
        out_shape=jax.ShapeDtypeStruct((t, d), F32),
        scratch_shapes=[pltpu.VMEM((tm + CONV_HALO, d), F32), pltpu.VMEM((tm, d), F32)],
        compiler_params=pltpu.CompilerParams(
            dimension_semantics=("arbitrary",), vmem_limit_bytes=V7X_VMEM_LIMIT_BYTES),
        name="dwconv_ln_out",
    )(u, u, x2d, w_dw, b_dw, ln_g, ln_b, w_out, b_out, gate)


def _ffn_kernel(*refs, has_attn, has_final):
    refs = list(refs)
    x_ref = refs.pop(0)
    if has_attn:
        o_ref_in, wo_ref, gate_m_ref = refs.pop(0), refs.pop(0), refs.pop(0)
    g_ref, shift_ref, scale_ref, gate_ref, win_ref, wout_ref = (refs.pop(0) for _ in range(6))
    if has_final:
        gf_ref = refs.pop(0)
    out_ref = refs.pop(0)

    x = x_ref[...]
    if has_attn:
        x = x + gate_m_ref[0] * jnp.dot(o_ref_in[...], wo_ref[...], preferred_element_type=F32)
    h = ((_rms(x) * g_ref[...]) * (1.0 + scale_ref[0]) + shift_ref[0]).astype(BF16)
    f = wout_ref.shape[0]
    ug = jnp.dot(h, win_ref[:, :f], preferred_element_type=F32)
    uu = jnp.dot(h, win_ref[:, f:], preferred_element_type=F32)
    a = (ug * jax.nn.sigmoid(ug) * uu).astype(BF16)
    x = x + gate_ref[0] * jnp.dot(a, wout_ref[...], preferred_element_type=F32)
    if has_final:
        x = _rms(x) * gf_ref[...]
    out_ref[...] = x


def _ffn(x2d, g, shift, scale, gate, w_in, w_out, *, seq, tm, attn=None, final_g=None):
    t, d = x2d.shape
    f = w_out.shape[0]
    tpb = seq // tm
    row = pl.BlockSpec((tm, d), lambda i: (i, 0))
    bvec = pl.BlockSpec((1, 1, d), lambda i: (i // tpb, 0, 0))
    args, specs = [x2d], [row]
    if attn is not None:
        o, w_o, gate_m = attn
        args += [o, w_o, gate_m]
        specs += [pl.BlockSpec((tm, o.shape[1]), lambda i: (i, 0)), _const_spec(w_o.shape), bvec]
    args += [g, shift, scale, gate, w_in, w_out]
    specs += [_const_spec((1, d)), bvec, bvec, bvec, _const_spec((d, 2 * f)), _const_spec((f, d))]
    if final_g is not None:
        args.append(final_g)
        specs.append(_const_spec((1, d)))
    kern = functools.partial(_ffn_kernel, has_attn=attn is not None, has_final=final_g is not None)
    return pl.pallas_call(
        kern,
        grid=(t // tm,),
        in_specs=specs,
        out_specs=row,
        out_shape=jax.ShapeDtypeStruct((t, d), F32),
        compiler_params=pltpu.CompilerParams(
            dimension_semantics=("arbitrary",), vmem_limit_bytes=V7X_VMEM_LIMIT_BYTES),
        name="swiglu_ffn",
    )(*args)


def _split3(x):
    hi = x.astype(BF16)
    r1 = x - hi.astype(F32)
    mid = r1.astype(BF16)
    lo = (r1 - mid.astype(F32)).astype(BF16)
    return hi, mid, lo


def _kvq_kernel(x_ref, gkv_ref, shkv_ref, sckv_ref, gq_ref, shq_ref, scq_ref,
                wk_ref, wvT_ref, wqT_ref, wf_ref, fb_ref,
                k_ref, kb_ref, vT_ref, qT_ref, cum_ref, carry_ref,
                *, tiles_per_batch, n_heads, q_scale):
    tm = x_ref.shape[0]
    i = pl.program_id(0)

    @pl.when((i % tiles_per_batch) == 0)
    def _():
        carry_ref[...] = jnp.zeros_like(carry_ref)

    r = _rms(x_ref[...])
    hk = ((r * gkv_ref[...]) * (1.0 + sckv_ref[0]) + shkv_ref[0]).astype(BF16)
    hq = ((r * gq_ref[...]) * (1.0 + scq_ref[0]) + shq_ref[0]).astype(BF16)

    k_ref[...] = jnp.dot(hk, wk_ref[...], preferred_element_type=F32).astype(BF16)
    vT_ref[0] = lax.dot_general(wvT_ref[...], hk, _NT, preferred_element_type=F32).astype(BF16)
    qT = lax.dot_general(wqT_ref[...], hq, _NT, preferred_element_type=F32)
    qT_ref[0] = (qT * q_scale).astype(BF16)

    fl = jnp.dot(hk, wf_ref[...], preferred_element_type=F32) + fb_ref[...]
    ls = jnp.minimum(fl, 0.0) - jnp.log1p(jnp.exp(-jnp.abs(fl)))
    row = lax.broadcasted_iota(jnp.int32, ls.shape, 0)
    sh = 1
    while sh < tm:
        ls = ls + jnp.where(row >= sh, pltpu.roll(ls, sh, axis=0), 0.0)
        sh *= 2
    cum = ls + carry_ref[...]
    carry_ref[...] = cum[tm - 1:tm, :]
    cum2 = cum * LOG2E
    cum_ref[...] = cum2[:, :n_heads]

    hi, mid, lo = (p.astype(F32) for p in _split3(cum2))
    lane = lax.broadcasted_iota(jnp.int32, cum.shape, 1)
    kb = jnp.where(lane < n_heads, -hi,
         jnp.where(lane < 2 * n_heads, -mid,
         jnp.where(lane < 3 * n_heads, -lo,
         jnp.where(lane < 3 * n_heads + 3, 1.0, 0.0))))
    kb_ref[...] = kb.astype(BF16)


def _kvq(x2d, gkv, shkv, sckv, gq, shq, scq, w_k, w_vT, w_qT, w_f, fb, *, batch, seq, tm,
         n_heads, q_scale):
    t, d = x2d.shape
    aw = w_k.shape[1]
    tpb = seq // tm
    row = lambda n: pl.BlockSpec((tm, n), lambda i: (i, 0))
    bvec = pl.BlockSpec((1, 1, d), lambda i: (i // tpb, 0, 0))
    colT = pl.BlockSpec((1, aw, tm), lambda i: (i // tpb, 0, i % tpb))
    kern = functools.partial(_kvq_kernel, tiles_per_batch=tpb, n_heads=n_heads, q_scale=q_scale)
    return pl.pallas_call(
        kern,
        grid=(t // tm,),
        in_specs=[
            row(d),
            _const_spec((1, d)), bvec, bvec,
            _const_spec((1, d)), bvec, bvec,
            _const_spec((d, aw)), _const_spec((aw, d)), _const_spec((aw, d)),
            _const_spec((d, LANES)), _const_spec((1, LANES)),
        ],
        out_specs=[row(aw), row(LANES), colT, colT, row(n_heads)],
        out_shape=[
            jax.ShapeDtypeStruct((t, aw), BF16),
            jax.ShapeDtypeStruct((t, LANES), BF16),
            jax.ShapeDtypeStruct((batch, aw, seq), BF16),
            jax.ShapeDtypeStruct((batch, aw, seq), BF16),
            jax.ShapeDtypeStruct((t, n_heads), F32),
        ],
        scratch_shapes=[pltpu.VMEM((1, LANES), F32)],
        compiler_params=pltpu.CompilerParams(
            dimension_semantics=("arbitrary",), vmem_limit_bytes=V7X_VMEM_LIMIT_BYTES),
        name="kvq_proj",
    )(x2d, gkv, shkv, sckv, gq, shq, scq, w_k, w_vT, w_qT, w_f, fb)


def _attn_kernel(qT_ref, cq_ref, k_ref, kb_ref, vT_ref, o_ref, s_ref, *, tq, tk, head_dim, n_heads):
    hd = head_dim
    n_diag = tq // tk
    assert n_diag % 2 == 0
    j = pl.program_id(1)
    qi = pl.program_id(2)

    row128 = lax.broadcasted_iota(jnp.int32, (LANES, tq), 0)
    causal = (lax.broadcasted_iota(jnp.int32, (tk, tk), 0)
              <= lax.broadcasted_iota(jnp.int32, (tk, tk), 1))

    q_augs = []
    for hh in range(2):
        h = 2 * j + hh
        qT = qT_ref[0, hh * hd:(hh + 1) * hd, :]
        zq = jnp.zeros_like(qT)
        top = jnp.concatenate([qT, zq] if hh == 0 else [zq, qT], axis=0)
        hi, mid, lo = _split3(cq_ref[0, 0, hh:hh + 1, :])
        pick = (row128 == h) | (row128 == n_heads + h) | (row128 == 2 * n_heads + h)
        bot = jnp.where(pick, 1.0, 0.0)
        bot = jnp.where(row128 == 3 * n_heads, hi.astype(F32), bot)
        bot = jnp.where(row128 == 3 * n_heads + 1, mid.astype(F32), bot)
        bot = jnp.where(row128 == 3 * n_heads + 2, lo.astype(F32), bot)
        q_augs.append(jnp.concatenate([top, bot.astype(BF16)], axis=0))

    def produce(slot, tile_idx, c0=0):
        ks = pl.multiple_of(tile_idx * tk, tk)
        k_aug = jnp.concatenate(
            [k_ref[0, pl.ds(ks, tk), :], kb_ref[0, pl.ds(ks, tk), :]], axis=1)
        maxes = []
        for hh in range(2):
            s = jnp.dot(k_aug, q_augs[hh][:, c0:], preferred_element_type=F32)
            s_ref[slot, hh, :, c0:] = s
            maxes.append(jnp.max(s, axis=0, keepdims=True))
        return tuple(maxes)

    def consume(state, hh, s, smax, k_start):
        m, l, acc = state
        m_new = jnp.maximum(m, smax)
        alpha = jnp.exp2(m - m_new)
        p = jnp.exp2(s - m_new)
        l = alpha * l + jnp.sum(p, axis=0, keepdims=True)
        v = vT_ref[0, hh * hd:(hh + 1) * hd, pl.ds(k_start, tk)]
        acc = alpha * acc + jnp.dot(v, p.astype(BF16), preferred_element_type=F32)
        return m_new, l, acc

    def step(carry, slot_in, slot_out, tile_idx):
        states, maxes = carry
        new_maxes = produce(slot_out, tile_idx + 1)
        ks = pl.multiple_of(tile_idx * tk, tk)
        states = tuple(consume(states[hh], hh, s_ref[slot_in, hh], maxes[hh], ks)
                       for hh in range(2))
        return states, new_maxes

    def two_steps(kk, carry):
        carry = step(carry, 0, 1, 2 * kk)
        return step(carry, 1, 0, 2 * kk + 1)

    init = tuple((jnp.full((1, tq), -jnp.inf, F32), jnp.zeros((1, tq), F32),
                  jnp.zeros((hd, tq), F32)) for _ in range(2))
    n_full = qi * n_diag
    states, _ = lax.fori_loop(0, n_full // 2, two_steps, (init, produce(0, 0)))
    states = list(states)

    for d in range(n_diag):
        c0 = d * tk
        if d + 1 < n_diag:
            produce((d + 1) % 2, n_full + d + 1, c0=c0 + tk)
        ks = pl.multiple_of((n_full + d) * tk, tk)
        for hh in range(2):
            s = jnp.where(causal, s_ref[d % 2, hh, :, c0:c0 + tk], -jnp.inf)
            if c0 + tk < tq:
                s = jnp.concatenate([s, s_ref[d % 2, hh, :, c0 + tk:]], axis=1)
            old = states[hh]
            new = consume(tuple(a[:, c0:] for a in old), hh, s,
                          jnp.max(s, axis=0, keepdims=True), ks)
            if c0:
                new = tuple(jnp.concatenate([a[:, :c0], b], axis=1) for a, b in zip(old, new))
            states[hh] = new

    outs = [acc / l for _, l, acc in states]
    o_ref[0] = jnp.concatenate(outs, axis=0).T.astype(BF16)


def _attention(qT, cqT, k, kb, vT, *, tq, tk, head_dim, n_heads):
    b, aw, s = qT.shape
    pair = 2 * head_dim
    assert pair == LANES
    kern = functools.partial(_attn_kernel, tq=tq, tk=tk, head_dim=head_dim, n_heads=n_heads)
    return pl.pallas_call(
        kern,
        grid=(b, n_heads // 2, s // tq),
        in_specs=[
            pl.BlockSpec((1, pair, tq), lambda bi, j, qi: (bi, j, qi)),
            pl.BlockSpec((1, 1, 2, tq), lambda bi, j, qi: (bi, j, 0, qi)),
            pl.BlockSpec((1, s, pair), lambda bi, j, qi: (bi, 0, j)),
            pl.BlockSpec((1, s, LANES), lambda bi, j, qi: (bi, 0, 0)),
            pl.BlockSpec((1, pair, s), lambda bi, j, qi: (bi, j, 0)),
        ],
        out_specs=pl.BlockSpec((1, tq, pair), lambda bi, j, qi: (bi, qi, j)),
        out_shape=jax.ShapeDtypeStruct((b, s, aw), BF16),
        scratch_shapes=[pltpu.VMEM((2, 2, tk, tq), F32)],
        compiler_params=pltpu.CompilerParams(
            dimension_semantics=("arbitrary", "arbitrary", "arbitrary"),
            vmem_limit_bytes=V7X_VMEM_LIMIT_BYTES),
        name="forgetting_attention",
    )(qT, cqT, k, kb, vT)


def kernel(x, c, mix_norm_g, mix_ada_w, mix_ada_b, ffn_norm_g, ffn_ada_w, ffn_ada_b, ffn_w_in, ffn_w_out, conv_w_in, conv_b_in, conv_w_dw, conv_b_dw, conv_ln_g, conv_ln_b, conv_w_out, conv_b_out, kv_norm_g, kv_ada_w, kv_ada_b, kv_w, forget_b, attn_w_q, attn_w_o, final_norm_g):
    batch, seq, d = x.shape
    depth = mix_norm_g.shape[0]
    n_a = conv_w_in.shape[0]
    n_heads = forget_b.shape[0]
    aw = attn_w_q.shape[-1]
    head_dim = aw // n_heads
    assert depth == 2 and n_a == 1 and attn_w_q.shape[0] == 1
    assert 3 * n_heads + 3 <= LANES
    t = batch * seq
    tm = ROW_TILE

    c_pad = jnp.zeros((8, d), F32).at[:batch].set(c)
    mix_ada = _ada(c_pad, mix_ada_w, mix_ada_b)[:, :batch]
    ffn_ada = _ada(c_pad, ffn_ada_w, ffn_ada_b)[:, :batch]
    kv_ada = _ada(c_pad, kv_ada_w[None], kv_ada_b[None])[0, :batch]

    def vecs(a, n):
        return [v.reshape(batch, 1, d) for v in jnp.split(a, n, axis=-1)]

    row = lambda v: v.reshape(1, -1)
    x2d = x.reshape(t, d)

    shift, scale, gate = vecs(mix_ada[0], 3)
    u = _conv_in(x2d, row(mix_norm_g[0]), shift, scale, conv_w_in[0].astype(BF16),
                 row(conv_b_in[0]), seq=seq, tm=CONV_IN_TILE, sub=CONV_IN_SUB)
    x2d = _dwconv(u, x2d, conv_w_dw[0], row(conv_b_dw[0]), row(conv_ln_g[0]), row(conv_ln_b[0]),
                  conv_w_out[0].astype(BF16), row(conv_b_out[0]), gate, seq=seq, tm=tm)
    shift, scale, gate = vecs(ffn_ada[0], 3)
    x2d = _ffn(x2d, row(ffn_norm_g[0]), shift, scale, gate, ffn_w_in[0].astype(BF16),
               ffn_w_out[0].astype(BF16), seq=seq, tm=tm)

    shkv, sckv = vecs(kv_ada, 2)
    shq, scq, gate_m = vecs(mix_ada[1], 3)
    w_k = kv_w[:, :aw].astype(BF16)
    w_vT = kv_w[:, aw:2 * aw].T.astype(BF16)
    w_qT = attn_w_q[0].T.astype(BF16)
    w_f = kv_w[:, 2 * aw:]
    pad = LANES - 3 * n_heads
    w_f_rep = jnp.concatenate([w_f, w_f, w_f, jnp.zeros((d, pad), F32)], axis=1).astype(BF16)
    fb_rep = jnp.concatenate([forget_b, forget_b, forget_b, jnp.zeros((pad,), F32)]).reshape(1, LANES)
    k, kb, vT, qT, cum = _kvq(x2d, row(kv_norm_g), shkv, sckv, row(mix_norm_g[1]), shq, scq,
                              w_k, w_vT, w_qT, w_f_rep, fb_rep, batch=batch, seq=seq, tm=tm,
                              n_heads=n_heads, q_scale=head_dim ** -0.5 * LOG2E)

    cqT = cum.reshape(batch, seq, n_heads // 2, 2).transpose(0, 2, 3, 1)
    o = _attention(qT, cqT, k.reshape(batch, seq, aw), kb.reshape(batch, seq, LANES), vT,
                   tq=ATTN_Q_TILE, tk=ATTN_K_TILE, head_dim=head_dim, n_heads=n_heads)
    shift, scale, gate = vecs(ffn_ada[1], 3)
    out = _ffn(x2d, row(ffn_norm_g[1]), shift, scale, gate, ffn_w_in[1].astype(BF16),
               ffn_w_out[1].astype(BF16), seq=seq, tm=tm,
               attn=(o.reshape(t, aw), attn_w_o[0].astype(BF16), gate_m),
               final_g=row(final_norm_g))
    return out.reshape(batch, seq, d)
```

```python
import functools

import jax
import jax.numpy as jnp
from jax import lax
from jax.experimental import pallas as pl
from jax.experimental.pallas import tpu as pltpu

F32 = jnp.float32
BF16 = jnp.bfloat16

EPS = 1e-6
V7X_VMEM_LIMIT_BYTES = 56 * 1024 * 1024
LANES = 128
CONV_HALO = 32

LOG2E = 1.4426950408889634

ROW_TILE = 512
CONV_IN_TILE = 1024
CONV_IN_SUB = 256
CONV_ROW_CHUNK = 256
ATTN_Q_TILE = 2048
ATTN_K_TILE = 512

_NT = (((1,), (1,)), ((), ()))


def _const_spec(shape):
    return pl.BlockSpec(shape, lambda *_: (0,) * len(shape), pipeline_mode=pl.Buffered(1))


def _rms(x):
    return x * lax.rsqrt(jnp.mean(x * x, axis=-1, keepdims=True) + EPS)


def _ada_kernel(c_ref, w_ref, b_ref, o_ref):
    c = c_ref[...]
    ca = c * jax.nn.sigmoid(c)
    o_ref[...] = jnp.dot(ca.astype(BF16), w_ref[...].astype(BF16),
                         preferred_element_type=F32) + b_ref[...]


def _ada(c_pad, w, b, tn=1024):
    n_l, d, n = w.shape
    rows = c_pad.shape[0]
    return pl.pallas_call(
        _ada_kernel,
        grid=(n_l, n // tn),
        in_specs=[
            pl.BlockSpec((rows, d), lambda l, j: (0, 0)),
            pl.BlockSpec((None, d, tn), lambda l, j: (l, 0, j)),
            pl.BlockSpec((None, 1, tn), lambda l, j: (l, 0, j)),
        ],
        out_specs=pl.BlockSpec((None, rows, tn), lambda l, j: (l, 0, j)),
        out_shape=jax.ShapeDtypeStruct((n_l, rows, n), F32),
        name="ada_proj",
    )(c_pad, w, b.reshape(n_l, 1, n))


def _conv_in_kernel(x_ref, g_ref, shift_ref, scale_ref, w_ref, b_ref, u_ref, *, sub):
    tm, d = u_ref.shape
    for r in range(0, tm, sub):
        rows = slice(r, r + sub)
        h = (_rms(x_ref[rows, :]) * g_ref[...]) * (1.0 + scale_ref[0]) + shift_ref[0]
        z = jnp.dot(h.astype(BF16), w_ref[...], preferred_element_type=F32) + b_ref[...]
        u_ref[rows, :] = z[:, :d] * jax.nn.sigmoid(z[:, d:])


def _conv_in(x2d, g, shift, scale, w_in, b_in, *, seq, tm, sub):
    t, d = x2d.shape
    tpb = seq // tm
    bvec = pl.BlockSpec((1, 1, d), lambda i: (i // tpb, 0, 0))
    return pl.pallas_call(
        functools.partial(_conv_in_kernel, sub=sub),
        grid=(t // tm,),
        in_specs=[
            pl.BlockSpec((tm, d), lambda i: (i, 0)),
            _const_spec((1, d)),
            bvec, bvec,
            _const_spec((d, 2 * d)),
            _const_spec((1, 2 * d)),
        ],
        out_specs=pl.BlockSpec((tm, d), lambda i: (i, 0)),
        out_shape=jax.ShapeDtypeStruct((t, d), F32),
        compiler_params=pltpu.CompilerParams(
            dimension_semantics=("arbitrary",), vmem_limit_bytes=V7X_VMEM_LIMIT_BYTES),
        name="conv_in_glu",
    )(x2d, g, shift, scale, w_in, b_in)


def _dwconv_kernel(u_ref, halo_ref, x_ref, wdw_ref, bdw_ref, lng_ref, lnb_ref,
                   wout_ref, bout_ref, gate_ref, o_ref, ext_ref, conv_ref,
                   *, tiles_per_batch, width, row_chunk, col_chunk):
    tm, d = u_ref.shape
    i = pl.program_id(0)
    first = (i % tiles_per_batch) == 0
    ext_ref[0:CONV_HALO, :] = jnp.where(first, 0.0, halo_ref[...])
    ext_ref[CONV_HALO:, :] = u_ref[...]

    off0 = CONV_HALO - (width - 1)

    def chunk(c, _):
        r0 = pl.multiple_of(c * row_chunk, row_chunk)
        for c0 in range(0, d, col_chunk):
            cols = slice(c0, c0 + col_chunk)
            out = jnp.zeros((row_chunk, col_chunk), F32)
            for s in range(8):
                rows = row_chunk + (8 if s else 0)
                part = None
                for k in range(width):
                    if (off0 + k) % 8 != s:
                        continue
                    q = (off0 + k) // 8
                    blk = ext_ref[pl.ds(r0 + 8 * q, rows), cols]
                    term = blk * wdw_ref[k:k + 1, cols]
                    part = term if part is None else part + term
                if part is None:
                    continue
                if s:
                    part = pltpu.roll(part, rows - s, axis=0)[:row_chunk]
                out = out + part
            conv_ref[pl.ds(r0, row_chunk), cols] = out
        return 0

    lax.fori_loop(0, tm // row_chunk, chunk, 0)

    v = conv_ref[...] + bdw_ref[...]
    mu = jnp.mean(v, axis=-1, keepdims=True)
    vc = v - mu
    var = jnp.mean(vc * vc, axis=-1, keepdims=True)
    y = vc * lax.rsqrt(var + EPS) * lng_ref[...] + lnb_ref[...]
    y = y * jax.nn.sigmoid(y)
    z = jnp.dot(y.astype(BF16), wout_ref[...], preferred_element_type=F32) + bout_ref[...]
    o_ref[...] = x_ref[...] + gate_ref[0] * z


def _dwconv(u, x2d, w_dw, b_dw, ln_g, ln_b, w_out, b_out, gate, *, seq, tm):
    t, d = u.shape
    width = w_dw.shape[0]
    assert width - 1 <= CONV_HALO and tm % CONV_HALO == 0
    tpb = seq // tm
    halo_blocks = tm // CONV_HALO
    kern = functools.partial(_dwconv_kernel, tiles_per_batch=tpb, width=width,
                             row_chunk=CONV_ROW_CHUNK, col_chunk=LANES)
    return pl.pallas_call(
        kern,
        grid=(t // tm,),
        in_specs=[
            pl.BlockSpec((tm, d), lambda i: (i, 0)),
            pl.BlockSpec((CONV_HALO, d), lambda i: (jnp.maximum(i * halo_blocks - 1, 0), 0)),
            pl.BlockSpec((tm, d), lambda i: (i, 0)),
            _const_spec((width, d)),
            _const_spec((1, d)), _const_spec((1, d)), _const_spec((1, d)),
            _const_spec((d, d)),
            _const_spec((1, d)),
            pl.BlockSpec((1, 1, d), lambda i: (i // tpb, 0, 0)),
        ],
        out_specs=pl.BlockSpec((tm, d), lambda i: (i, 0)),
        out_shape=jax.ShapeDtypeStruct((t, d), F32),
        scratch_shapes=[pltpu.VMEM((tm + CONV_HALO, d), F32), pltpu.VMEM((tm, d), F32)],
        compiler_params=pltpu.CompilerParams(
            dimension_semantics=("arbitrary",), vmem_limit_bytes=V7X_VMEM_LIMIT_BYTES),
        name="dwconv_ln_out",
    )(u, u, x2d, w_dw, b_dw, ln_g, ln_b, w_out, b_out, gate)


def _ffn_kernel(*refs, has_attn, has_final):
    refs = list(refs)
    x_ref = refs.pop(0)
    if has_attn:
        o_ref_in, wo_ref, gate_m_ref = refs.pop(0), refs.pop(0), refs.pop(0)
    g_ref, shift_ref, scale_ref, gate_ref, win_ref, wout_ref = (refs.pop(0) for _ in range(6))
    if has_final:
        gf_ref = refs.pop(0)
    out_ref = refs.pop(0)

    x = x_ref[...]
    if has_attn:
        x = x + gate_m_ref[0] * jnp.dot(o_ref_in[...], wo_ref[...], preferred_element_type=F32)
    h = ((_rms(x) * g_ref[...]) * (1.0 + scale_ref[0]) + shift_ref[0]).astype(BF16)
    f = wout_ref.shape[0]
    ug = jnp.dot(h, win_ref[:, :f], preferred_element_type=F32)
    uu = jnp.dot(h, win_ref[:, f:], preferred_element_type=F32)
    a = (ug * jax.nn.sigmoid(ug) * uu).astype(BF16)
    x = x + gate_ref[0] * jnp.dot(a, wout_ref[...], preferred_element_type=F32)
    if has_final:
        x = _rms(x) * gf_ref[...]
    out_ref[...] = x


def _ffn(x2d, g, shift, scale, gate, w_in, w_out, *, seq, tm, attn=None, final_g=None):
    t, d = x2d.shape
    f = w_out.shape[0]
    tpb = seq // tm
    row = pl.BlockSpec((tm, d), lambda i: (i, 0))
    bvec = pl.BlockSpec((1, 1, d), lambda i: (i // tpb, 0, 0))
    args, specs = [x2d], [row]
    if attn is not None:
        o, w_o, gate_m = attn
        args += [o, w_o, gate_m]
        specs += [pl.BlockSpec((tm, o.shape[1]), lambda i: (i, 0)), _const_spec(w_o.shape), bvec]
    args += [g, shift, scale, gate, w_in, w_out]
    specs += [_const_spec((1, d)), bvec, bvec, bvec, _const_spec((d, 2 * f)), _const_spec((f, d))]
    if final_g is not None:
        args.append(final_g)
        specs.append(_const_spec((1, d)))
    kern = functools.partial(_ffn_kernel, has_attn=attn is not None, has_final=final_g is not None)
    return pl.pallas_call(
        kern,
        grid=(t // tm,),
        in_specs=specs,
        out_specs=row,
        out_shape=jax.ShapeDtypeStruct((t, d), F32),
        compiler_params=pltpu.CompilerParams(
            dimension_semantics=("arbitrary",), vmem_limit_bytes=V7X_VMEM_LIMIT_BYTES),
        name="swiglu_ffn",
    )(*args)


def _split3(x):
    hi = x.astype(BF16)
    r1 = x - hi.astype(F32)
    mid = r1.astype(BF16)
    lo = (r1 - mid.astype(F32)).astype(BF16)
    return hi, mid, lo


def _kvq_kernel(x_ref, gkv_ref, shkv_ref, sckv_ref, gq_ref, shq_ref, scq_ref,
                wk_ref, wvT_ref, wqT_ref, wf_ref, fb_ref,
                k_ref, kb_ref, vT_ref, qT_ref, cum_ref, carry_ref,
                *, tiles_per_batch, n_heads, q_scale):
    tm = x_ref.shape[0]
    i = pl.program_id(0)

    @pl.when((i % tiles_per_batch) == 0)
    def _():
        carry_ref[...] = jnp.zeros_like(carry_ref)

    r = _rms(x_ref[...])
    hk = ((r * gkv_ref[...]) * (1.0 + sckv_ref[0]) + shkv_ref[0]).astype(BF16)
    hq = ((r * gq_ref[...]) * (1.0 + scq_ref[0]) + shq_ref[0]).astype(BF16)

    k_ref[...] = jnp.dot(hk, wk_ref[...], preferred_element_type=F32).astype(BF16)
    vT_ref[0] = lax.dot_general(wvT_ref[...], hk, _NT, preferred_element_type=F32).astype(BF16)
    qT = lax.dot_general(wqT_ref[...], hq, _NT, preferred_element_type=F32)
    qT_ref[0] = (qT * q_scale).astype(BF16)

    fl = jnp.dot(hk, wf_ref[...], preferred_element_type=F32) + fb_ref[...]
    ls = jnp.minimum(fl, 0.0) - jnp.log1p(jnp.exp(-jnp.abs(fl)))
    row = lax.broadcasted_iota(jnp.int32, ls.shape, 0)
    sh = 1
    while sh < tm:
        ls = ls + jnp.where(row >= sh, pltpu.roll(ls, sh, axis=0), 0.0)
        sh *= 2
    cum = ls + carry_ref[...]
    carry_ref[...] = cum[tm - 1:tm, :]
    cum2 = cum * LOG2E
    cum_ref[...] = cum2[:, :n_heads]

    hi, mid, lo = (p.astype(F32) for p in _split3(cum2))
    lane = lax.broadcasted_iota(jnp.int32, cum.shape, 1)
    kb = jnp.where(lane < n_heads, -hi,
         jnp.where(lane < 2 * n_heads, -mid,
         jnp.where(lane < 3 * n_heads, -lo,
         jnp.where(lane < 3 * n_heads + 3, 1.0, 0.0))))
    kb_ref[...] = kb.astype(BF16)


def _kvq(x2d, gkv, shkv, sckv, gq, shq, scq, w_k, w_vT, w_qT, w_f, fb, *, batch, seq, tm,
         n_heads, q_scale):
    t, d = x2d.shape
    aw = w_k.shape[1]
    tpb = seq // tm
    row = lambda n: pl.BlockSpec((tm, n), lambda i: (i, 0))
    bvec = pl.BlockSpec((1, 1, d), lambda i: (i // tpb, 0, 0))
    colT = pl.BlockSpec((1, aw, tm), lambda i: (i // tpb, 0, i % tpb))
    kern = functools.partial(_kvq_kernel, tiles_per_batch=tpb, n_heads=n_heads, q_scale=q_scale)
    return pl.pallas_call(
        kern,
        grid=(t // tm,),
        in_specs=[
            row(d),
            _const_spec((1, d)), bvec, bvec,
            _const_spec((1, d)), bvec, bvec,
            _const_spec((d, aw)), _const_spec((aw, d)), _const_spec((aw, d)),
            _const_spec((d, LANES)), _const_spec((1, LANES)),
        ],
        out_specs=[row(aw), row(LANES), colT, colT, row(n_heads)],
        out_shape=[
            jax.ShapeDtypeStruct((t, aw), BF16),
            jax.ShapeDtypeStruct((t, LANES), BF16),
            jax.ShapeDtypeStruct((batch, aw, seq), BF16),
            jax.ShapeDtypeStruct((batch, aw, seq), BF16),
            jax.ShapeDtypeStruct((t, n_heads), F32),
        ],
        scratch_shapes=[pltpu.VMEM((1, LANES), F32)],
        compiler_params=pltpu.CompilerParams(
            dimension_semantics=("arbitrary",), vmem_limit_bytes=V7X_VMEM_LIMIT_BYTES),
        name="kvq_proj",
    )(x2d, gkv, shkv, sckv, gq, shq, scq, w_k, w_vT, w_qT, w_f, fb)


def _attn_kernel(qT_ref, cq_ref, k_ref, kb_ref, vT_ref, o_ref, s_ref, *, tq, tk, head_dim, n_heads):
    hd = head_dim
    n_diag = tq // tk
    assert n_diag % 2 == 0
    j = pl.program_id(1)
    qi = pl.program_id(2)

    row128 = lax.broadcasted_iota(jnp.int32, (LANES, tq), 0)
    causal = (lax.broadcasted_iota(jnp.int32, (tk, tk), 0)
              <= lax.broadcasted_iota(jnp.int32, (tk, tk), 1))

    q_augs = []
    for hh in range(2):
        h = 2 * j + hh
        qT = qT_ref[0, hh * hd:(hh + 1) * hd, :]
        zq = jnp.zeros_like(qT)
        top = jnp.concatenate([qT, zq] if hh == 0 else [zq, qT], axis=0)
        hi, mid, lo = _split3(cq_ref[0, 0, hh:hh + 1, :])
        pick = (row128 == h) | (row128 == n_heads + h) | (row128 == 2 * n_heads + h)
        bot = jnp.where(pick, 1.0, 0.0)
        bot = jnp.where(row128 == 3 * n_heads, hi.astype(F32), bot)
        bot = jnp.where(row128 == 3 * n_heads + 1, mid.astype(F32), bot)
        bot = jnp.where(row128 == 3 * n_heads + 2, lo.astype(F32), bot)
        q_augs.append(jnp.concatenate([top, bot.astype(BF16)], axis=0))

    def produce(slot, tile_idx, c0=0):
        ks = pl.multiple_of(tile_idx * tk, tk)
        k_aug = jnp.concatenate(
            [k_ref[0, pl.ds(ks, tk), :], kb_ref[0, pl.ds(ks, tk), :]], axis=1)
        maxes = []
        for hh in range(2):
            s = jnp.dot(k_aug, q_augs[hh][:, c0:], preferred_element_type=F32)
            s_ref[slot, hh, :, c0:] = s
            maxes.append(jnp.max(s, axis=0, keepdims=True))
        return tuple(maxes)

    def consume(state, hh, s, smax, k_start):
        m, l, acc = state
        m_new = jnp.maximum(m, smax)
        alpha = jnp.exp2(m - m_new)
        p = jnp.exp2(s - m_new)
        l = alpha * l + jnp.sum(p, axis=0, keepdims=True)
        v = vT_ref[0, hh * hd:(hh + 1) * hd, pl.ds(k_start, tk)]
        acc = alpha * acc + jnp.dot(v, p.astype(BF16), preferred_element_type=F32)
        return m_new, l, acc

    def step(carry, slot_in, slot_out, tile_idx):
        states, maxes = carry
        new_maxes = produce(slot_out, tile_idx + 1)
        ks = pl.multiple_of(tile_idx * tk, tk)
        states = tuple(consume(states[hh], hh, s_ref[slot_in, hh], maxes[hh], ks)
                       for hh in range(2))
        return states, new_maxes

    def two_steps(kk, carry):
        carry = step(carry, 0, 1, 2 * kk)
        return step(carry, 1, 0, 2 * kk + 1)

    init = tuple((jnp.full((1, tq), -jnp.inf, F32), jnp.zeros((1, tq), F32),
                  jnp.zeros((hd, tq), F32)) for _ in range(2))
    n_full = qi * n_diag
    states, _ = lax.fori_loop(0, n_full // 2, two_steps, (init, produce(0, 0)))
    states = list(states)

    for d in range(n_diag):
        c0 = d * tk
        if d + 1 < n_diag:
            produce((d + 1) % 2, n_full + d + 1, c0=c0 + tk)
        ks = pl.multiple_of((n_full + d) * tk, tk)
        for hh in range(2):
            s = jnp.where(causal, s_ref[d % 2, hh, :, c0:c0 + tk], -jnp.inf)
            if c0 + tk < tq:
                s = jnp.concatenate([s, s_ref[d % 2, hh, :, c0 + tk:]], axis=1)
            old = states[hh]
            new = consume(tuple(a[:, c0:] for a in old), hh, s,
                          jnp.max(s, axis=0, keepdims=True), ks)
            if c0:
                new = tuple(jnp.concatenate([a[:, :c0], b], axis=1) for a, b in zip(old, new))
            states[hh] = new

    outs = [acc / l for _, l, acc in states]
    o_ref[0] = jnp.concatenate(outs, axis=0).T.astype(BF16)


def _attention(qT, cqT, k, kb, vT, *, tq, tk, head_dim, n_heads):
    b, aw, s = qT.shape
    pair = 2 * head_dim
    assert pair == LANES
    kern = functools.partial(_attn_kernel, tq=tq, tk=tk, head_dim=head_dim, n_heads=n_heads)
    return pl.pallas_call(
        kern,
        grid=(b, n_heads // 2, s // tq),
        in_specs=[
            pl.BlockSpec((1, pair, tq), lambda bi, j, qi: (bi, j, qi)),
            pl.BlockSpec((1, 1, 2, tq), lambda bi, j, qi: (bi, j, 0, qi)),
            pl.BlockSpec((1, s, pair), lambda bi, j, qi: (bi, 0, j)),
            pl.BlockSpec((1, s, LANES), lambda bi, j, qi: (bi, 0, 0)),
            pl.BlockSpec((1, pair, s), lambda bi, j, qi: (bi, j, 0)),
        ],
        out_specs=pl.BlockSpec((1, tq, pair), lambda bi, j, qi: (bi, qi, j)),
        out_shape=jax.ShapeDtypeStruct((b, s, aw), BF16),
        scratch_shapes=[pltpu.VMEM((2, 2, tk, tq), F32)],
        compiler_params=pltpu.CompilerParams(
            dimension_semantics=("arbitrary", "arbitrary", "arbitrary"),
            vmem_limit_bytes=V7X_VMEM_LIMIT_BYTES),
        name="forgetting_attention",
    )(qT, cqT, k, kb, vT)


def kernel(x, c, mix_norm_g, mix_ada_w, mix_ada_b, ffn_norm_g, ffn_ada_w, ffn_ada_b, ffn_w_in, ffn_w_out, conv_w_in, conv_b_in, conv_w_dw, conv_b_dw, conv_ln_g, conv_ln_b, conv_w_out, conv_b_out, kv_norm_g, kv_ada_w, kv_ada_b, kv_w, forget_b, attn_w_q, attn_w_o, final_norm_g):
    batch, seq, d = x.shape
    depth = mix_norm_g.shape[0]
    n_a = conv_w_in.shape[0]
    n_heads = forget_b.shape[0]
    aw = attn_w_q.shape[-1]
    head_dim = aw // n_heads
    assert depth == 2 and n_a == 1 and attn_w_q.shape[0] == 1
    assert 3 * n_heads + 3 <= LANES
    t = batch * seq
    tm = ROW_TILE

    c_pad = jnp.zeros((8, d), F32).at[:batch].set(c)
    mix_ada = _ada(c_pad, mix_ada_w, mix_ada_b)[:, :batch]
    ffn_ada = _ada(c_pad, ffn_ada_w, ffn_ada_b)[:, :batch]
    kv_ada = _ada(c_pad, kv_ada_w[None], kv_ada_b[None])[0, :batch]

    def vecs(a, n):
        return [v.reshape(batch, 1, d) for v in jnp.split(a, n, axis=-1)]

    row = lambda v: v.reshape(1, -1)
    x2d = x.reshape(t, d)

    shift, scale, gate = vecs(mix_ada[0], 3)
    u = _conv_in(x2d, row(mix_norm_g[0]), shift, scale, conv_w_in[0].astype(BF16),
                 row(conv_b_in[0]), seq=seq, tm=CONV_IN_TILE, sub=CONV_IN_SUB)
    x2d = _dwconv(u, x2d, conv_w_dw[0], row(conv_b_dw[0]), row(conv_ln_g[0]), row(conv_ln_b[0]),
                  conv_w_out[0].astype(BF16), row(conv_b_out[0]), gate, seq=seq, tm=tm)
    shift, scale, gate = vecs(ffn_ada[0], 3)
    x2d = _ffn(x2d, row(ffn_norm_g[0]), shift, scale, gate, ffn_w_in[0].astype(BF16),
               ffn_w_out[0].astype(BF16), seq=seq, tm=tm)

    shkv, sckv = vecs(kv_ada, 2)
    shq, scq, gate_m = vecs(mix_ada[1], 3)
    w_k = kv_w[:, :aw].astype(BF16)
    w_vT = kv_w[:, aw:2 * aw].T.astype(BF16)
    w_qT = attn_w_q[0].T.astype(BF16)
    w_f = kv_w[:, 2 * aw:]
    pad = LANES - 3 * n_heads
    w_f_rep = jnp.concatenate([w_f, w_f, w_f, jnp.zeros((d, pad), F32)], axis=1).astype(BF16)
    fb_rep = jnp.concatenate([forget_b, forget_b, forget_b, jnp.zeros((pad,), F32)]).reshape(1, LANES)
    k, kb, vT, qT, cum = _kvq(x2d, row(kv_norm_g), shkv, sckv, row(mix_norm_g[1]), shq, scq,
                              w_k, w_vT, w_qT, w_f_rep, fb_rep, batch=batch, seq=seq, tm=tm,
                              n_heads=n_heads, q_scale=head_dim ** -0.5 * LOG2E)

    cqT = cum.reshape(batch, seq, n_heads // 2, 2).transpose(0, 2, 3, 1)
    o = _attention(qT, cqT, k.reshape(batch, seq, aw), kb.reshape(batch, seq, LANES), vT,
                   tq=ATTN_Q_TILE, tk=ATTN_K_TILE, head_dim=head_dim, n_heads=n_heads)
    shift, scale, gate = vecs(ffn_ada[1], 3)
    out = _ffn(x2d, row(ffn_norm_g[1]), shift, scale, gate, ffn_w_in[1].astype(BF16),
               ffn_w_out[1].astype(BF16), seq=seq, tm=tm,
               attn=(o.reshape(t, aw), attn_w_o[0].astype(BF16), gate_m),
               final_g=row(final_norm_g))
    return out.reshape(batch, seq, d)
```

```python
import functools

import jax
import jax.numpy as jnp
from jax import lax
from jax.experimental import pallas as pl
from jax.experimental.pallas import tpu as pltpu

F32 = jnp.float32
BF16 = jnp.bfloat16

EPS = 1e-6
V7X_VMEM_LIMIT_BYTES = 56 * 1024 * 1024
LANES = 128
CONV_HALO = 32

LOG2E = 1.4426950408889634

ROW_TILE = 512
CONV_IN_TILE = 1024
CONV_IN_SUB = 256
FFN_SUB = 256
CONV_ROW_CHUNK = 256
ATTN_Q_TILE = 2048
ATTN_K_TILE = 512

_NT = (((1,), (1,)), ((), ()))


def _const_spec(shape):
    return pl.BlockSpec(shape, lambda *_: (0,) * len(shape), pipeline_mode=pl.Buffered(1))


def _rms(x):
    return x * lax.rsqrt(jnp.mean(x * x, axis=-1, keepdims=True) + EPS)


def _ada_kernel(c_ref, w_ref, b_ref, o_ref):
    c = c_ref[...]
    ca = c * jax.nn.sigmoid(c)
    o_ref[...] = jnp.dot(ca.astype(BF16), w_ref[...].astype(BF16),
                         preferred_element_type=F32) + b_ref[...]


def _ada(c_pad, w, b, tn=1024):
    n_l, d, n = w.shape
    rows = c_pad.shape[0]
    return pl.pallas_call(
        _ada_kernel,
        grid=(n_l, n // tn),
        in_specs=[
            pl.BlockSpec((rows, d), lambda l, j: (0, 0)),
            pl.BlockSpec((None, d, tn), lambda l, j: (l, 0, j)),
            pl.BlockSpec((None, 1, tn), lambda l, j: (l, 0, j)),
        ],
        out_specs=pl.BlockSpec((None, rows, tn), lambda l, j: (l, 0, j)),
        out_shape=jax.ShapeDtypeStruct((n_l, rows, n), F32),
        name="ada_proj",
    )(c_pad, w, b.reshape(n_l, 1, n))


def _conv_in_kernel(x_ref, g_ref, shift_ref, scale_ref, w_ref, b_ref, u_ref, *, sub):
    tm, d = u_ref.shape
    for r in range(0, tm, sub):
        rows = slice(r, r + sub)
        h = (_rms(x_ref[rows, :]) * g_ref[...]) * (1.0 + scale_ref[0]) + shift_ref[0]
        z = jnp.dot(h.astype(BF16), w_ref[...], preferred_element_type=F32) + b_ref[...]
        u_ref[rows, :] = z[:, :d] * jax.nn.sigmoid(z[:, d:])


def _conv_in(x2d, g, shift, scale, w_in, b_in, *, seq, tm, sub):
    t, d = x2d.shape
    tpb = seq // tm
    bvec = pl.BlockSpec((1, 1, d), lambda i: (i // tpb, 0, 0))
    return pl.pallas_call(
        functools.partial(_conv_in_kernel, sub=sub),
        grid=(t // tm,),
        in_specs=[
            pl.BlockSpec((tm, d), lambda i: (i, 0)),
            _const_spec((1, d)),
            bvec, bvec,
            _const_spec((d, 2 * d)),
            _const_spec((1, 2 * d)),
        ],
        out_specs=pl.BlockSpec((tm, d), lambda i: (i, 0)),
        out_shape=jax.ShapeDtypeStruct((t, d), F32),
        compiler_params=pltpu.CompilerParams(
            dimension_semantics=("arbitrary",), vmem_limit_bytes=V7X_VMEM_LIMIT_BYTES),
        name="conv_in_glu",
    )(x2d, g, shift, scale, w_in, b_in)


def _dwconv_kernel(u_ref, halo_ref, x_ref, wdw_ref, bdw_ref, lng_ref, lnb_ref,
                   wout_ref, bout_ref, gate_ref, o_ref, ext_ref, conv_ref,
                   *, tiles_per_batch, width, row_chunk, col_chunk):
    tm, d = u_ref.shape
    i = pl.program_id(0)
    first = (i % tiles_per_batch) == 0
    ext_ref[0:CONV_HALO, :] = jnp.where(first, 0.0, halo_ref[...])
    ext_ref[CONV_HALO:, :] = u_ref[...]

    off0 = CONV_HALO - (width - 1)

    def chunk(c, _):
        r0 = pl.multiple_of(c * row_chunk, row_chunk)
        for c0 in range(0, d, col_chunk):
            cols = slice(c0, c0 + col_chunk)
            out = jnp.zeros((row_chunk, col_chunk), F32)
            for s in range(8):
                rows = row_chunk + (8 if s else 0)
                part = None
                for k in range(width):
                    if (off0 + k) % 8 != s:
                        continue
                    q = (off0 + k) // 8
                    blk = ext_ref[pl.ds(r0 + 8 * q, rows), cols]
                    term = blk * wdw_ref[k:k + 1, cols]
                    part = term if part is None else part + term
                if part is None:
                    continue
                if s:
                    part = pltpu.roll(part, rows - s, axis=0)[:row_chunk]
                out = out + part
            conv_ref[pl.ds(r0, row_chunk), cols] = out
        return 0

    lax.fori_loop(0, tm // row_chunk, chunk, 0)

    v = conv_ref[...] + bdw_ref[...]
    mu = jnp.mean(v, axis=-1, keepdims=True)
    vc = v - mu
    var = jnp.mean(vc * vc, axis=-1, keepdims=True)
    y = vc * lax.rsqrt(var + EPS) * lng_ref[...] + lnb_ref[...]
    y = y * jax.nn.sigmoid(y)
    z = jnp.dot(y.astype(BF16), wout_ref[...], preferred_element_type=F32) + bout_ref[...]
    o_ref[...] = x_ref[...] + gate_ref[0] * z


def _dwconv(u, x2d, w_dw, b_dw, ln_g, ln_b, w_out, b_out, gate, *, seq, tm):
    t, d = u.shape
    width = w_dw.shape[0]
    assert width - 1 <= CONV_HALO and tm % CONV_HALO == 0
    tpb = seq // tm
    halo_blocks = tm // CONV_HALO
    kern = functools.partial(_dwconv_kernel, tiles_per_batch=tpb, width=width,
                             row_chunk=CONV_ROW_CHUNK, col_chunk=LANES)
    return pl.pallas_call(
        kern,
        grid=(t // tm,),
        in_specs=[
            pl.BlockSpec((tm, d), lambda i: (i, 0)),
            pl.BlockSpec((CONV_HALO, d), lambda i: (jnp.maximum(i * halo_blocks - 1, 0), 0)),
            pl.BlockSpec((tm, d), lambda i: (i, 0)),
            _const_spec((width, d)),
            _const_spec((1, d)), _const_spec((1, d)), _const_spec((1, d)),
            _const_spec((d, d)),
            _const_spec((1, d)),
            pl.BlockSpec((1, 1, d), lambda i: (i // tpb, 0, 0)),
        ],
        out_specs=pl.BlockSpec((tm, d), lambda i: (i, 0)),
        out_shape=jax.ShapeDtypeStruct((t, d), F32),
        scratch_shapes=[pltpu.VMEM((tm + CONV_HALO, d), F32), pltpu.VMEM((tm, d), F32)],
        compiler_params=pltpu.CompilerParams(
            dimension_semantics=("arbitrary",), vmem_limit_bytes=V7X_VMEM_LIMIT_BYTES),
        name="dwconv_ln_out",
    )(u, u, x2d, w_dw, b_dw, ln_g, ln_b, w_out, b_out, gate)


def _ffn_kernel(*refs, has_attn, has_final, sub):
    refs = list(refs)
    x_ref = refs.pop(0)
    if has_attn:
        o_ref_in, wo_ref, gate_m_ref = refs.pop(0), refs.pop(0), refs.pop(0)
    g_ref, shift_ref, scale_ref, gate_ref, win_ref, wout_ref = (refs.pop(0) for _ in range(6))
    if has_final:
        gf_ref = refs.pop(0)
    out_ref = refs.pop(0)

    f = wout_ref.shape[0]
    for r in range(0, x_ref.shape[0], sub):
        rows = slice(r, r + sub)
        x = x_ref[rows, :]
        if has_attn:
            x = x + gate_m_ref[0] * jnp.dot(o_ref_in[rows, :], wo_ref[...],
                                            preferred_element_type=F32)
        h = ((_rms(x) * g_ref[...]) * (1.0 + scale_ref[0]) + shift_ref[0]).astype(BF16)
        ug = jnp.dot(h, win_ref[:, :f], preferred_element_type=F32)
        uu = jnp.dot(h, win_ref[:, f:], preferred_element_type=F32)
        a = (ug * jax.nn.sigmoid(ug) * uu).astype(BF16)
        x = x + gate_ref[0] * jnp.dot(a, wout_ref[...], preferred_element_type=F32)
        if has_final:
            x = _rms(x) * gf_ref[...]
        out_ref[rows, :] = x


def _ffn(x2d, g, shift, scale, gate, w_in, w_out, *, seq, tm, attn=None, final_g=None):
    t, d = x2d.shape
    f = w_out.shape[0]
    tpb = seq // tm
    row = pl.BlockSpec((tm, d), lambda i: (i, 0))
    bvec = pl.BlockSpec((1, 1, d), lambda i: (i // tpb, 0, 0))
    args, specs = [x2d], [row]
    if attn is not None:
        o, w_o, gate_m = attn
        args += [o, w_o, gate_m]
        specs += [pl.BlockSpec((tm, o.shape[1]), lambda i: (i, 0)), _const_spec(w_o.shape), bvec]
    args += [g, shift, scale, gate, w_in, w_out]
    specs += [_const_spec((1, d)), bvec, bvec, bvec, _const_spec((d, 2 * f)), _const_spec((f, d))]
    if final_g is not None:
        args.append(final_g)
        specs.append(_const_spec((1, d)))
    kern = functools.partial(_ffn_kernel, has_attn=attn is not None, has_final=final_g is not None,
                             sub=FFN_SUB)
    return pl.pallas_call(
        kern,
        grid=(t // tm,),
        in_specs=specs,
        out_specs=row,
        out_shape=jax.ShapeDtypeStruct((t, d), F32),
        compiler_params=pltpu.CompilerParams(
            dimension_semantics=("arbitrary",), vmem_limit_bytes=V7X_VMEM_LIMIT_BYTES),
        name="swiglu_ffn",
    )(*args)


def _split3(x):
    hi = x.astype(BF16)
    r1 = x - hi.astype(F32)
    mid = r1.astype(BF16)
    lo = (r1 - mid.astype(F32)).astype(BF16)
    return hi, mid, lo


def _kvq_kernel(x_ref, gkv_ref, shkv_ref, sckv_ref, gq_ref, shq_ref, scq_ref,
                wk_ref, wvT_ref, wqT_ref, wf_ref, fb_ref,
                k_ref, kb_ref, vT_ref, qT_ref, cum_ref, carry_ref,
                *, tiles_per_batch, n_heads, q_scale):
    tm = x_ref.shape[0]
    i = pl.program_id(0)

    @pl.when((i % tiles_per_batch) == 0)
    def _():
        carry_ref[...] = jnp.zeros_like(carry_ref)

    r = _rms(x_ref[...])
    hk = ((r * gkv_ref[...]) * (1.0 + sckv_ref[0]) + shkv_ref[0]).astype(BF16)
    hq = ((r * gq_ref[...]) * (1.0 + scq_ref[0]) + shq_ref[0]).astype(BF16)

    k_ref[...] = jnp.dot(hk, wk_ref[...], preferred_element_type=F32).astype(BF16)
    vT_ref[0] = lax.dot_general(wvT_ref[...], hk, _NT, preferred_element_type=F32).astype(BF16)
    qT = lax.dot_general(wqT_ref[...], hq, _NT, preferred_element_type=F32)
    qT_ref[0] = (qT * q_scale).astype(BF16)

    fl = jnp.dot(hk, wf_ref[...], preferred_element_type=F32) + fb_ref[...]
    ls = jnp.minimum(fl, 0.0) - jnp.log1p(jnp.exp(-jnp.abs(fl)))
    row = lax.broadcasted_iota(jnp.int32, ls.shape, 0)
    sh = 1
    while sh < tm:
        ls = ls + jnp.where(row >= sh, pltpu.roll(ls, sh, axis=0), 0.0)
        sh *= 2
    cum = ls + carry_ref[...]
    carry_ref[...] = cum[tm - 1:tm, :]
    cum2 = cum * LOG2E
    cum_ref[...] = cum2[:, :n_heads]

    hi, mid, lo = (p.astype(F32) for p in _split3(cum2))
    lane = lax.broadcasted_iota(jnp.int32, cum.shape, 1)
    kb = jnp.where(lane < n_heads, -hi,
         jnp.where(lane < 2 * n_heads, -mid,
         jnp.where(lane < 3 * n_heads, -lo,
         jnp.where(lane < 3 * n_heads + 3, 1.0, 0.0))))
    kb_ref[...] = kb.astype(BF16)


def _kvq(x2d, gkv, shkv, sckv, gq, shq, scq, w_k, w_vT, w_qT, w_f, fb, *, batch, seq, tm,
         n_heads, q_scale):
    t, d = x2d.shape
    aw = w_k.shape[1]
    tpb = seq // tm
    row = lambda n: pl.BlockSpec((tm, n), lambda i: (i, 0))
    bvec = pl.BlockSpec((1, 1, d), lambda i: (i // tpb, 0, 0))
    colT = pl.BlockSpec((1, aw, tm), lambda i: (i // tpb, 0, i % tpb))
    kern = functools.partial(_kvq_kernel, tiles_per_batch=tpb, n_heads=n_heads, q_scale=q_scale)
    return pl.pallas_call(
        kern,
        grid=(t // tm,),
        in_specs=[
            row(d),
            _const_spec((1, d)), bvec, bvec,
            _const_spec((1, d)), bvec, bvec,
            _const_spec((d, aw)), _const_spec((aw, d)), _const_spec((aw, d)),
            _const_spec((d, LANES)), _const_spec((1, LANES)),
        ],
        out_specs=[row(aw), row(LANES), colT, colT, row(n_heads)],
        out_shape=[
            jax.ShapeDtypeStruct((t, aw), BF16),
            jax.ShapeDtypeStruct((t, LANES), BF16),
            jax.ShapeDtypeStruct((batch, aw, seq), BF16),
            jax.ShapeDtypeStruct((batch, aw, seq), BF16),
            jax.ShapeDtypeStruct((t, n_heads), F32),
        ],
        scratch_shapes=[pltpu.VMEM((1, LANES), F32)],
        compiler_params=pltpu.CompilerParams(
            dimension_semantics=("arbitrary",), vmem_limit_bytes=V7X_VMEM_LIMIT_BYTES),
        name="kvq_proj",
    )(x2d, gkv, shkv, sckv, gq, shq, scq, w_k, w_vT, w_qT, w_f, fb)


def _attn_kernel(qT_ref, cq_ref, k_ref, kb_ref, vT_ref, o_ref, s_ref, *, tq, tk, head_dim, n_heads):
    hd = head_dim
    n_diag = tq // tk
    assert n_diag % 2 == 0
    j = pl.program_id(1)
    qi = pl.program_id(2)

    row128 = lax.broadcasted_iota(jnp.int32, (LANES, tq), 0)
    causal = (lax.broadcasted_iota(jnp.int32, (tk, tk), 0)
              <= lax.broadcasted_iota(jnp.int32, (tk, tk), 1))

    q_augs = []
    for hh in range(2):
        h = 2 * j + hh
        qT = qT_ref[0, hh * hd:(hh + 1) * hd, :]
        zq = jnp.zeros_like(qT)
        top = jnp.concatenate([qT, zq] if hh == 0 else [zq, qT], axis=0)
        hi, mid, lo = _split3(cq_ref[0, 0, hh:hh + 1, :])
        pick = (row128 == h) | (row128 == n_heads + h) | (row128 == 2 * n_heads + h)
        bot = jnp.where(pick, 1.0, 0.0)
        bot = jnp.where(row128 == 3 * n_heads, hi.astype(F32), bot)
        bot = jnp.where(row128 == 3 * n_heads + 1, mid.astype(F32), bot)
        bot = jnp.where(row128 == 3 * n_heads + 2, lo.astype(F32), bot)
        q_augs.append(jnp.concatenate([top, bot.astype(BF16)], axis=0))

    def produce(slot, tile_idx, c0=0):
        ks = pl.multiple_of(tile_idx * tk, tk)
        k_aug = jnp.concatenate(
            [k_ref[0, pl.ds(ks, tk), :], kb_ref[0, pl.ds(ks, tk), :]], axis=1)
        maxes = []
        for hh in range(2):
            s = jnp.dot(k_aug, q_augs[hh][:, c0:], preferred_element_type=F32)
            s_ref[slot, hh, :, c0:] = s
            maxes.append(jnp.max(s, axis=0, keepdims=True))
        return tuple(maxes)

    def consume(state, hh, s, smax, k_start):
        m, l, acc = state
        m_new = jnp.maximum(m, smax)
        alpha = jnp.exp2(m - m_new)
        p = jnp.exp2(s - m_new)
        l = alpha * l + jnp.sum(p, axis=0, keepdims=True)
        v = vT_ref[0, hh * hd:(hh + 1) * hd, pl.ds(k_start, tk)]
        acc = alpha * acc + jnp.dot(v, p.astype(BF16), preferred_element_type=F32)
        return m_new, l, acc

    def step(carry, slot_in, slot_out, tile_idx):
        states, maxes = carry
        new_maxes = produce(slot_out, tile_idx + 1)
        ks = pl.multiple_of(tile_idx * tk, tk)
        states = tuple(consume(states[hh], hh, s_ref[slot_in, hh], maxes[hh], ks)
                       for hh in range(2))
        return states, new_maxes

    def two_steps(kk, carry):
        carry = step(carry, 0, 1, 2 * kk)
        return step(carry, 1, 0, 2 * kk + 1)

    init = tuple((jnp.full((1, tq), -jnp.inf, F32), jnp.zeros((1, tq), F32),
                  jnp.zeros((hd, tq), F32)) for _ in range(2))
    n_full = qi * n_diag
    states, _ = lax.fori_loop(0, n_full // 2, two_steps, (init, produce(0, 0)))
    states = list(states)

    for d in range(n_diag):
        c0 = d * tk
        if d + 1 < n_diag:
            produce((d + 1) % 2, n_full + d + 1, c0=c0 + tk)
        ks = pl.multiple_of((n_full + d) * tk, tk)
        for hh in range(2):
            s = jnp.where(causal, s_ref[d % 2, hh, :, c0:c0 + tk], -jnp.inf)
            if c0 + tk < tq:
                s = jnp.concatenate([s, s_ref[d % 2, hh, :, c0 + tk:]], axis=1)
            old = states[hh]
            new = consume(tuple(a[:, c0:] for a in old), hh, s,
                          jnp.max(s, axis=0, keepdims=True), ks)
            if c0:
                new = tuple(jnp.concatenate([a[:, :c0], b], axis=1) for a, b in zip(old, new))
            states[hh] = new

    outs = [acc / l for _, l, acc in states]
    o_ref[0] = jnp.concatenate(outs, axis=0).T.astype(BF16)


def _attention(qT, cqT, k, kb, vT, *, tq, tk, head_dim, n_heads):
    b, aw, s = qT.shape
    pair = 2 * head_dim
    assert pair == LANES
    kern = functools.partial(_attn_kernel, tq=tq, tk=tk, head_dim=head_dim, n_heads=n_heads)
    return pl.pallas_call(
        kern,
        grid=(b, n_heads // 2, s // tq),
        in_specs=[
            pl.BlockSpec((1, pair, tq), lambda bi, j, qi: (bi, j, qi)),
            pl.BlockSpec((1, 1, 2, tq), lambda bi, j, qi: (bi, j, 0, qi)),
            pl.BlockSpec((1, s, pair), lambda bi, j, qi: (bi, 0, j)),
            pl.BlockSpec((1, s, LANES), lambda bi, j, qi: (bi, 0, 0)),
            pl.BlockSpec((1, pair, s), lambda bi, j, qi: (bi, j, 0)),
        ],
        out_specs=pl.BlockSpec((1, tq, pair), lambda bi, j, qi: (bi, qi, j)),
        out_shape=jax.ShapeDtypeStruct((b, s, aw), BF16),
        scratch_shapes=[pltpu.VMEM((2, 2, tk, tq), F32)],
        compiler_params=pltpu.CompilerParams(
            dimension_semantics=("arbitrary", "arbitrary", "arbitrary"),
            vmem_limit_bytes=V7X_VMEM_LIMIT_BYTES),
        name="forgetting_attention",
    )(qT, cqT, k, kb, vT)


def kernel(x, c, mix_norm_g, mix_ada_w, mix_ada_b, ffn_norm_g, ffn_ada_w, ffn_ada_b, ffn_w_in, ffn_w_out, conv_w_in, conv_b_in, conv_w_dw, conv_b_dw, conv_ln_g, conv_ln_b, conv_w_out, conv_b_out, kv_norm_g, kv_ada_w, kv_ada_b, kv_w, forget_b, attn_w_q, attn_w_o, final_norm_g):
    batch, seq, d = x.shape
    depth = mix_norm_g.shape[0]
    n_a = conv_w_in.shape[0]
    n_heads = forget_b.shape[0]
    aw = attn_w_q.shape[-1]
    head_dim = aw // n_heads
    assert depth == 2 and n_a == 1 and attn_w_q.shape[0] == 1
    assert 3 * n_heads + 3 <= LANES
    t = batch * seq
    tm = ROW_TILE

    c_pad = jnp.zeros((8, d), F32).at[:batch].set(c)
    mix_ada = _ada(c_pad, mix_ada_w, mix_ada_b)[:, :batch]
    ffn_ada = _ada(c_pad, ffn_ada_w, ffn_ada_b)[:, :batch]
    kv_ada = _ada(c_pad, kv_ada_w[None], kv_ada_b[None])[0, :batch]

    def vecs(a, n):
        return [v.reshape(batch, 1, d) for v in jnp.split(a, n, axis=-1)]

    row = lambda v: v.reshape(1, -1)
    x2d = x.reshape(t, d)

    shift, scale, gate = vecs(mix_ada[0], 3)
    u = _conv_in(x2d, row(mix_norm_g[0]), shift, scale, conv_w_in[0].astype(BF16),
                 row(conv_b_in[0]), seq=seq, tm=CONV_IN_TILE, sub=CONV_IN_SUB)
    x2d = _dwconv(u, x2d, conv_w_dw[0], row(conv_b_dw[0]), row(conv_ln_g[0]), row(conv_ln_b[0]),
                  conv_w_out[0].astype(BF16), row(conv_b_out[0]), gate, seq=seq, tm=tm)
    shift, scale, gate = vecs(ffn_ada[0], 3)
    x2d = _ffn(x2d, row(ffn_norm_g[0]), shift, scale, gate, ffn_w_in[0].astype(BF16),
               ffn_w_out[0].astype(BF16), seq=seq, tm=tm)

    shkv, sckv = vecs(kv_ada, 2)
    shq, scq, gate_m = vecs(mix_ada[1], 3)
    w_k = kv_w[:, :aw].astype(BF16)
    w_vT = kv_w[:, aw:2 * aw].T.astype(BF16)
    w_qT = attn_w_q[0].T.astype(BF16)
    w_f = kv_w[:, 2 * aw:]
    pad = LANES - 3 * n_heads
    w_f_rep = jnp.concatenate([w_f, w_f, w_f, jnp.zeros((d, pad), F32)], axis=1).astype(BF16)
    fb_rep = jnp.concatenate([forget_b, forget_b, forget_b, jnp.zeros((pad,), F32)]).reshape(1, LANES)
    k, kb, vT, qT, cum = _kvq(x2d, row(kv_norm_g), shkv, sckv, row(mix_norm_g[1]), shq, scq,
                              w_k, w_vT, w_qT, w_f_rep, fb_rep, batch=batch, seq=seq, tm=tm,
                              n_heads=n_heads, q_scale=head_dim ** -0.5 * LOG2E)

    cqT = cum.reshape(batch, seq, n_heads // 2, 2).transpose(0, 2, 3, 1)
    o = _attention(qT, cqT, k.reshape(batch, seq, aw), kb.reshape(batch, seq, LANES), vT,
                   tq=ATTN_Q_TILE, tk=ATTN_K_TILE, head_dim=head_dim, n_heads=n_heads)
    shift, scale, gate = vecs(ffn_ada[1], 3)
    out = _ffn(x2d, row(ffn_norm_g[1]), shift, scale, gate, ffn_w_in[1].astype(BF16),
               ffn_w_out[1].astype(BF16), seq=seq, tm=tm,
               attn=(o.reshape(t, aw), attn_w_o[0].astype(BF16), gate_m),
               final_g=row(final_norm_g))
    return out.reshape(batch, seq, d)
```

```python
import functools

import jax
import jax.numpy as jnp
from jax import lax
from jax.experimental import pallas as pl
from jax.experimental.pallas import tpu as pltpu

F32 = jnp.float32
BF16 = jnp.bfloat16

EPS = 1e-6
V7X_VMEM_LIMIT_BYTES = 56 * 1024 * 1024
LANES = 128
BF16_SUBLANES = 16
CONV_HALO = 32

LOG2E = 1.4426950408889634

ROW_TILE = 512
CONV_IN_TILE = 1024
CONV_IN_SUB = 256
FFN_SUB = 256
CONV_ROW_CHUNK = 256
ATTN_Q_TILE = 2048
ATTN_K_TILE = 512

_NT = (((1,), (1,)), ((), ()))


def _const_spec(shape):
    return pl.BlockSpec(shape, lambda *_: (0,) * len(shape), pipeline_mode=pl.Buffered(1))


def _rms(x):
    return x * lax.rsqrt(jnp.mean(x * x, axis=-1, keepdims=True) + EPS)


def _ada_kernel(c_ref, w_ref, b_ref, o_ref):
    c = c_ref[...]
    ca = c * jax.nn.sigmoid(c)
    o_ref[...] = jnp.dot(ca.astype(BF16), w_ref[...].astype(BF16),
                         preferred_element_type=F32) + b_ref[...]


def _ada(c_pad, w, b, tn=1024):
    n_l, d, n = w.shape
    rows = c_pad.shape[0]
    return pl.pallas_call(
        _ada_kernel,
        grid=(n_l, n // tn),
        in_specs=[
            pl.BlockSpec((rows, d), lambda l, j: (0, 0)),
            pl.BlockSpec((None, d, tn), lambda l, j: (l, 0, j)),
            pl.BlockSpec((None, 1, tn), lambda l, j: (l, 0, j)),
        ],
        out_specs=pl.BlockSpec((None, rows, tn), lambda l, j: (l, 0, j)),
        out_shape=jax.ShapeDtypeStruct((n_l, rows, n), F32),
        name="ada_proj",
    )(c_pad, w, b.reshape(n_l, 1, n))


def _conv_in_kernel(x_ref, g_ref, shift_ref, scale_ref, w_ref, b_ref, u_ref, *, sub):
    tm, d = u_ref.shape
    for r in range(0, tm, sub):
        rows = slice(r, r + sub)
        h = (_rms(x_ref[rows, :]) * g_ref[...]) * (1.0 + scale_ref[0]) + shift_ref[0]
        z = jnp.dot(h.astype(BF16), w_ref[...], preferred_element_type=F32) + b_ref[...]
        u_ref[rows, :] = z[:, :d] * jax.nn.sigmoid(z[:, d:])


def _conv_in(x2d, g, shift, scale, w_in, b_in, *, seq, tm, sub):
    t, d = x2d.shape
    tpb = seq // tm
    bvec = pl.BlockSpec((1, 1, d), lambda i: (i // tpb, 0, 0))
    return pl.pallas_call(
        functools.partial(_conv_in_kernel, sub=sub),
        grid=(t // tm,),
        in_specs=[
            pl.BlockSpec((tm, d), lambda i: (i, 0)),
            _const_spec((1, d)),
            bvec, bvec,
            _const_spec((d, 2 * d)),
            _const_spec((1, 2 * d)),
        ],
        out_specs=pl.BlockSpec((tm, d), lambda i: (i, 0)),
        out_shape=jax.ShapeDtypeStruct((t, d), F32),
        compiler_params=pltpu.CompilerParams(
            dimension_semantics=("arbitrary",), vmem_limit_bytes=V7X_VMEM_LIMIT_BYTES),
        name="conv_in_glu",
    )(x2d, g, shift, scale, w_in, b_in)


def _dwconv_kernel(u_ref, halo_ref, x_ref, wdw_ref, bdw_ref, lng_ref, lnb_ref,
                   wout_ref, bout_ref, gate_ref, o_ref, ext_ref, conv_ref,
                   *, tiles_per_batch, width, row_chunk, col_chunk):
    tm, d = u_ref.shape
    i = pl.program_id(0)
    first = (i % tiles_per_batch) == 0
    ext_ref[0:CONV_HALO, :] = jnp.where(first, 0.0, halo_ref[...])
    ext_ref[CONV_HALO:, :] = u_ref[...]

    off0 = CONV_HALO - (width - 1)

    def chunk(c, _):
        r0 = pl.multiple_of(c * row_chunk, row_chunk)
        for c0 in range(0, d, col_chunk):
            cols = slice(c0, c0 + col_chunk)
            out = jnp.zeros((row_chunk, col_chunk), F32)
            for s in range(8):
                rows = row_chunk + (8 if s else 0)
                part = None
                for k in range(width):
                    if (off0 + k) % 8 != s:
                        continue
                    q = (off0 + k) // 8
                    blk = ext_ref[pl.ds(r0 + 8 * q, rows), cols]
                    term = blk * wdw_ref[k:k + 1, cols]
                    part = term if part is None else part + term
                if part is None:
                    continue
                if s:
                    part = pltpu.roll(part, rows - s, axis=0)[:row_chunk]
                out = out + part
            conv_ref[pl.ds(r0, row_chunk), cols] = out
        return 0

    lax.fori_loop(0, tm // row_chunk, chunk, 0)

    v = conv_ref[...] + bdw_ref[...]
    mu = jnp.mean(v, axis=-1, keepdims=True)
    vc = v - mu
    var = jnp.mean(vc * vc, axis=-1, keepdims=True)
    y = vc * lax.rsqrt(var + EPS) * lng_ref[...] + lnb_ref[...]
    y = y * jax.nn.sigmoid(y)
    z = jnp.dot(y.astype(BF16), wout_ref[...], preferred_element_type=F32) + bout_ref[...]
    o_ref[...] = x_ref[...] + gate_ref[0] * z


def _dwconv(u, x2d, w_dw, b_dw, ln_g, ln_b, w_out, b_out, gate, *, seq, tm):
    t, d = u.shape
    width = w_dw.shape[0]
    assert width - 1 <= CONV_HALO and tm % CONV_HALO == 0
    tpb = seq // tm
    halo_blocks = tm // CONV_HALO
    kern = functools.partial(_dwconv_kernel, tiles_per_batch=tpb, width=width,
                             row_chunk=CONV_ROW_CHUNK, col_chunk=LANES)
    return pl.pallas_call(
        kern,
        grid=(t // tm,),
        in_specs=[
            pl.BlockSpec((tm, d), lambda i: (i, 0)),
            pl.BlockSpec((CONV_HALO, d), lambda i: (jnp.maximum(i * halo_blocks - 1, 0), 0)),
            pl.BlockSpec((tm, d), lambda i: (i, 0)),
            _const_spec((width, d)),
            _const_spec((1, d)), _const_spec((1, d)), _const_spec((1, d)),
            _const_spec((d, d)),
            _const_spec((1, d)),
            pl.BlockSpec((1, 1, d), lambda i: (i // tpb, 0, 0)),
        ],
        out_specs=pl.BlockSpec((tm, d), lambda i: (i, 0)),
        out_shape=jax.ShapeDtypeStruct((t, d), F32),
        scratch_shapes=[pltpu.VMEM((tm + CONV_HALO, d), F32), pltpu.VMEM((tm, d), F32)],
        compiler_params=pltpu.CompilerParams(
            dimension_semantics=("arbitrary",), vmem_limit_bytes=V7X_VMEM_LIMIT_BYTES),
        name="dwconv_ln_out",
    )(u, u, x2d, w_dw, b_dw, ln_g, ln_b, w_out, b_out, gate)


def _ffn_kernel(*refs, has_attn, has_final, sub):
    refs = list(refs)
    x_ref = refs.pop(0)
    if has_attn:
        o_ref_in, wo_ref, gate_m_ref = refs.pop(0), refs.pop(0), refs.pop(0)
    g_ref, shift_ref, scale_ref, gate_ref, win_ref, wout_ref = (refs.pop(0) for _ in range(6))
    if has_final:
        gf_ref = refs.pop(0)
    out_ref = refs.pop(0)

    f = wout_ref.shape[0]
    for r in range(0, x_ref.shape[0], sub):
        rows = slice(r, r + sub)
        x = x_ref[rows, :]
        if has_attn:
            x = x + gate_m_ref[0] * jnp.dot(o_ref_in[rows, :], wo_ref[...],
                                            preferred_element_type=F32)
        h = ((_rms(x) * g_ref[...]) * (1.0 + scale_ref[0]) + shift_ref[0]).astype(BF16)
        ug = jnp.dot(h, win_ref[:, :f], preferred_element_type=F32)
        uu = jnp.dot(h, win_ref[:, f:], preferred_element_type=F32)
        a = (ug * jax.nn.sigmoid(ug) * uu).astype(BF16)
        x = x + gate_ref[0] * jnp.dot(a, wout_ref[...], preferred_element_type=F32)
        if has_final:
            x = _rms(x) * gf_ref[...]
        out_ref[rows, :] = x


def _ffn(x2d, g, shift, scale, gate, w_in, w_out, *, seq, tm, attn=None, final_g=None):
    t, d = x2d.shape
    f = w_out.shape[0]
    tpb = seq // tm
    row = pl.BlockSpec((tm, d), lambda i: (i, 0))
    bvec = pl.BlockSpec((1, 1, d), lambda i: (i // tpb, 0, 0))
    args, specs = [x2d], [row]
    if attn is not None:
        o, w_o, gate_m = attn
        args += [o, w_o, gate_m]
        specs += [pl.BlockSpec((tm, o.shape[1]), lambda i: (i, 0)), _const_spec(w_o.shape), bvec]
    args += [g, shift, scale, gate, w_in, w_out]
    specs += [_const_spec((1, d)), bvec, bvec, bvec, _const_spec((d, 2 * f)), _const_spec((f, d))]
    if final_g is not None:
        args.append(final_g)
        specs.append(_const_spec((1, d)))
    kern = functools.partial(_ffn_kernel, has_attn=attn is not None, has_final=final_g is not None,
                             sub=tm if attn is not None else FFN_SUB)
    return pl.pallas_call(
        kern,
        grid=(t // tm,),
        in_specs=specs,
        out_specs=row,
        out_shape=jax.ShapeDtypeStruct((t, d), F32),
        compiler_params=pltpu.CompilerParams(
            dimension_semantics=("arbitrary",), vmem_limit_bytes=V7X_VMEM_LIMIT_BYTES),
        name="swiglu_ffn",
    )(*args)


def _split3(x):
    hi = x.astype(BF16)
    r1 = x - hi.astype(F32)
    mid = r1.astype(BF16)
    lo = (r1 - mid.astype(F32)).astype(BF16)
    return hi, mid, lo


def _kvq_kernel(x_ref, gkv_ref, shkv_ref, sckv_ref, gq_ref, shq_ref, scq_ref,
                wk_ref, wvT_ref, wqT_ref, wf_ref, fb_ref,
                k_ref, kb_ref, vT_ref, qT_ref, cum_ref, carry_ref,
                *, tiles_per_batch, n_heads, q_scale):
    tm = x_ref.shape[0]
    i = pl.program_id(0)

    @pl.when((i % tiles_per_batch) == 0)
    def _():
        carry_ref[...] = jnp.zeros_like(carry_ref)

    r = _rms(x_ref[...])
    hk = ((r * gkv_ref[...]) * (1.0 + sckv_ref[0]) + shkv_ref[0]).astype(BF16)
    hq = ((r * gq_ref[...]) * (1.0 + scq_ref[0]) + shq_ref[0]).astype(BF16)

    k_ref[...] = jnp.dot(hk, wk_ref[...], preferred_element_type=F32).astype(BF16)
    vT_ref[0] = lax.dot_general(wvT_ref[...], hk, _NT, preferred_element_type=F32).astype(BF16)
    qT = lax.dot_general(wqT_ref[...], hq, _NT, preferred_element_type=F32)
    qT_ref[0] = (qT * q_scale).astype(BF16)

    fl = jnp.dot(hk, wf_ref[...], preferred_element_type=F32) + fb_ref[...]
    ls = jnp.minimum(fl, 0.0) - jnp.log1p(jnp.exp(-jnp.abs(fl)))
    row = lax.broadcasted_iota(jnp.int32, ls.shape, 0)
    sh = 1
    while sh < tm:
        ls = ls + jnp.where(row >= sh, pltpu.roll(ls, sh, axis=0), 0.0)
        sh *= 2
    cum = ls + carry_ref[...]
    carry_ref[...] = cum[tm - 1:tm, :]
    cum2 = cum * LOG2E
    cum_ref[...] = cum2[:, :n_heads]

    hi, mid, lo = (p.astype(F32) for p in _split3(cum2))
    lane = lax.broadcasted_iota(jnp.int32, cum.shape, 1)
    kb = jnp.where(lane < n_heads, -hi,
         jnp.where(lane < 2 * n_heads, -mid,
         jnp.where(lane < 3 * n_heads, -lo,
         jnp.where(lane < 3 * n_heads + 3, 1.0, 0.0))))
    kb_ref[...] = kb.astype(BF16)


def _kvq(x2d, gkv, shkv, sckv, gq, shq, scq, w_k, w_vT, w_qT, w_f, fb, *, batch, seq, tm,
         n_heads, q_scale):
    t, d = x2d.shape
    aw = w_k.shape[1]
    tpb = seq // tm
    row = lambda n: pl.BlockSpec((tm, n), lambda i: (i, 0))
    bvec = pl.BlockSpec((1, 1, d), lambda i: (i // tpb, 0, 0))
    colT = pl.BlockSpec((1, aw, tm), lambda i: (i // tpb, 0, i % tpb))
    kern = functools.partial(_kvq_kernel, tiles_per_batch=tpb, n_heads=n_heads, q_scale=q_scale)
    return pl.pallas_call(
        kern,
        grid=(t // tm,),
        in_specs=[
            row(d),
            _const_spec((1, d)), bvec, bvec,
            _const_spec((1, d)), bvec, bvec,
            _const_spec((d, aw)), _const_spec((aw, d)), _const_spec((aw, d)),
            _const_spec((d, LANES)), _const_spec((1, LANES)),
        ],
        out_specs=[row(aw), row(LANES), colT, colT, row(n_heads)],
        out_shape=[
            jax.ShapeDtypeStruct((t, aw), BF16),
            jax.ShapeDtypeStruct((t, LANES), BF16),
            jax.ShapeDtypeStruct((batch, aw, seq), BF16),
            jax.ShapeDtypeStruct((batch, aw, seq), BF16),
            jax.ShapeDtypeStruct((t, n_heads), F32),
        ],
        scratch_shapes=[pltpu.VMEM((1, LANES), F32)],
        compiler_params=pltpu.CompilerParams(
            dimension_semantics=("arbitrary",), vmem_limit_bytes=V7X_VMEM_LIMIT_BYTES),
        name="kvq_proj",
    )(x2d, gkv, shkv, sckv, gq, shq, scq, w_k, w_vT, w_qT, w_f, fb)


def _attn_kernel(qT_ref, cq_ref, k_ref, kb_ref, vT_ref, o_ref, s_ref, *, tq, tk, head_dim, n_heads):
    hd = head_dim
    n_diag = tq // tk
    assert n_diag % 2 == 0
    j = pl.program_id(1)
    qi = pl.program_id(2)

    row128 = lax.broadcasted_iota(jnp.int32, (LANES, tq), 0)
    causal = (lax.broadcasted_iota(jnp.int32, (tk, tk), 0)
              <= lax.broadcasted_iota(jnp.int32, (tk, tk), 1))

    q_augs = []
    for hh in range(2):
        h = 2 * j + hh
        qT = qT_ref[0, hh * hd:(hh + 1) * hd, :]
        zq = jnp.zeros_like(qT)
        top = jnp.concatenate([qT, zq] if hh == 0 else [zq, qT], axis=0)
        hi, mid, lo = _split3(cq_ref[0, 0, hh:hh + 1, :])
        pick = (row128 == h) | (row128 == n_heads + h) | (row128 == 2 * n_heads + h)
        bot = jnp.where(pick, 1.0, 0.0)
        bot = jnp.where(row128 == 3 * n_heads, hi.astype(F32), bot)
        bot = jnp.where(row128 == 3 * n_heads + 1, mid.astype(F32), bot)
        bot = jnp.where(row128 == 3 * n_heads + 2, lo.astype(F32), bot)
        q_augs.append(jnp.concatenate([top, bot.astype(BF16)], axis=0))

    def produce(slot, tile_idx, c0=0):
        ks = pl.multiple_of(tile_idx * tk, tk)
        k_aug = jnp.concatenate(
            [k_ref[0, pl.ds(ks, tk), :], kb_ref[0, pl.ds(ks, tk), :]], axis=1)
        maxes = []
        for hh in range(2):
            s = jnp.dot(k_aug, q_augs[hh][:, c0:], preferred_element_type=F32)
            s_ref[slot, hh, :, c0:] = s
            maxes.append(jnp.max(s, axis=0, keepdims=True))
        return tuple(maxes)

    def consume(state, hh, s, smax, k_start):
        m, acc = state
        m_new = jnp.maximum(m, smax)
        alpha = jnp.exp2(m - m_new)
        p = jnp.exp2(s - m_new).astype(BF16)
        v = vT_ref[0, hh * hd:(hh + 1) * hd, pl.ds(k_start, tk)]
        v_aug = jnp.concatenate([v, jnp.ones((BF16_SUBLANES, tk), BF16)], axis=0)
        acc = alpha * acc + jnp.dot(v_aug, p, preferred_element_type=F32)
        return m_new, acc

    def step(carry, slot_in, slot_out, tile_idx):
        states, maxes = carry
        new_maxes = produce(slot_out, tile_idx + 1)
        ks = pl.multiple_of(tile_idx * tk, tk)
        states = tuple(consume(states[hh], hh, s_ref[slot_in, hh], maxes[hh], ks)
                       for hh in range(2))
        return states, new_maxes

    def two_steps(kk, carry):
        carry = step(carry, 0, 1, 2 * kk)
        return step(carry, 1, 0, 2 * kk + 1)

    init = tuple((jnp.full((1, tq), -jnp.inf, F32),
                  jnp.zeros((hd + BF16_SUBLANES, tq), F32)) for _ in range(2))
    n_full = qi * n_diag
    states, _ = lax.fori_loop(0, n_full // 2, two_steps, (init, produce(0, 0)))
    states = list(states)

    for d in range(n_diag):
        c0 = d * tk
        if d + 1 < n_diag:
            produce((d + 1) % 2, n_full + d + 1, c0=c0 + tk)
        ks = pl.multiple_of((n_full + d) * tk, tk)
        for hh in range(2):
            s = jnp.where(causal, s_ref[d % 2, hh, :, c0:c0 + tk], -jnp.inf)
            if c0 + tk < tq:
                s = jnp.concatenate([s, s_ref[d % 2, hh, :, c0 + tk:]], axis=1)
            old = states[hh]
            new = consume(tuple(a[:, c0:] for a in old), hh, s,
                          jnp.max(s, axis=0, keepdims=True), ks)
            if c0:
                new = tuple(jnp.concatenate([a[:, :c0], b], axis=1) for a, b in zip(old, new))
            states[hh] = new

    outs = [acc[:hd] / acc[hd:hd + 1] for _, acc in states]
    o_ref[0] = jnp.concatenate(outs, axis=0).T.astype(BF16)


def _attention(qT, cqT, k, kb, vT, *, tq, tk, head_dim, n_heads):
    b, aw, s = qT.shape
    pair = 2 * head_dim
    assert pair == LANES
    kern = functools.partial(_attn_kernel, tq=tq, tk=tk, head_dim=head_dim, n_heads=n_heads)
    return pl.pallas_call(
        kern,
        grid=(b, n_heads // 2, s // tq),
        in_specs=[
            pl.BlockSpec((1, pair, tq), lambda bi, j, qi: (bi, j, qi)),
            pl.BlockSpec((1, 1, 2, tq), lambda bi, j, qi: (bi, j, 0, qi)),
            pl.BlockSpec((1, s, pair), lambda bi, j, qi: (bi, 0, j)),
            pl.BlockSpec((1, s, LANES), lambda bi, j, qi: (bi, 0, 0)),
            pl.BlockSpec((1, pair, s), lambda bi, j, qi: (bi, j, 0)),
        ],
        out_specs=pl.BlockSpec((1, tq, pair), lambda bi, j, qi: (bi, qi, j)),
        out_shape=jax.ShapeDtypeStruct((b, s, aw), BF16),
        scratch_shapes=[pltpu.VMEM((2, 2, tk, tq), F32)],
        compiler_params=pltpu.CompilerParams(
            dimension_semantics=("arbitrary", "arbitrary", "arbitrary"),
            vmem_limit_bytes=V7X_VMEM_LIMIT_BYTES),
        name="forgetting_attention",
    )(qT, cqT, k, kb, vT)


def kernel(x, c, mix_norm_g, mix_ada_w, mix_ada_b, ffn_norm_g, ffn_ada_w, ffn_ada_b, ffn_w_in, ffn_w_out, conv_w_in, conv_b_in, conv_w_dw, conv_b_dw, conv_ln_g, conv_ln_b, conv_w_out, conv_b_out, kv_norm_g, kv_ada_w, kv_ada_b, kv_w, forget_b, attn_w_q, attn_w_o, final_norm_g):
    batch, seq, d = x.shape
    depth = mix_norm_g.shape[0]
    n_a = conv_w_in.shape[0]
    n_heads = forget_b.shape[0]
    aw = attn_w_q.shape[-1]
    head_dim = aw // n_heads
    assert depth == 2 and n_a == 1 and attn_w_q.shape[0] == 1
    assert 3 * n_heads + 3 <= LANES
    t = batch * seq
    tm = ROW_TILE

    c_pad = jnp.zeros((8, d), F32).at[:batch].set(c)
    mix_ada = _ada(c_pad, mix_ada_w, mix_ada_b)[:, :batch]
    ffn_ada = _ada(c_pad, ffn_ada_w, ffn_ada_b)[:, :batch]
    kv_ada = _ada(c_pad, kv_ada_w[None], kv_ada_b[None])[0, :batch]

    def vecs(a, n):
        return [v.reshape(batch, 1, d) for v in jnp.split(a, n, axis=-1)]

    row = lambda v: v.reshape(1, -1)
    x2d = x.reshape(t, d)

    shift, scale, gate = vecs(mix_ada[0], 3)
    u = _conv_in(x2d, row(mix_norm_g[0]), shift, scale, conv_w_in[0].astype(BF16),
                 row(conv_b_in[0]), seq=seq, tm=CONV_IN_TILE, sub=CONV_IN_SUB)
    x2d = _dwconv(u, x2d, conv_w_dw[0], row(conv_b_dw[0]), row(conv_ln_g[0]), row(conv_ln_b[0]),
                  conv_w_out[0].astype(BF16), row(conv_b_out[0]), gate, seq=seq, tm=tm)
    shift, scale, gate = vecs(ffn_ada[0], 3)
    x2d = _ffn(x2d, row(ffn_norm_g[0]), shift, scale, gate, ffn_w_in[0].astype(BF16),
               ffn_w_out[0].astype(BF16), seq=seq, tm=tm)

    shkv, sckv = vecs(kv_ada, 2)
    shq, scq, gate_m = vecs(mix_ada[1], 3)
    w_k = kv_w[:, :aw].astype(BF16)
    w_vT = kv_w[:, aw:2 * aw].T.astype(BF16)
    w_qT = attn_w_q[0].T.astype(BF16)
    w_f = kv_w[:, 2 * aw:]
    pad = LANES - 3 * n_heads
    w_f_rep = jnp.concatenate([w_f, w_f, w_f, jnp.zeros((d, pad), F32)], axis=1).astype(BF16)
    fb_rep = jnp.concatenate([forget_b, forget_b, forget_b, jnp.zeros((pad,), F32)]).reshape(1, LANES)
    k, kb, vT, qT, cum = _kvq(x2d, row(kv_norm_g), shkv, sckv, row(mix_norm_g[1]), shq, scq,
                              w_k, w_vT, w_qT, w_f_rep, fb_rep, batch=batch, seq=seq, tm=tm,
                              n_heads=n_heads, q_scale=head_dim ** -0.5 * LOG2E)

    cqT = cum.reshape(batch, seq, n_heads // 2, 2).transpose(0, 2, 3, 1)
    o = _attention(qT, cqT, k.reshape(batch, seq, aw), kb.reshape(batch, seq, LANES), vT,
                   tq=ATTN_Q_TILE, tk=ATTN_K_TILE, head_dim=head_dim, n_heads=n_heads)
    shift, scale, gate = vecs(ffn_ada[1], 3)
    out = _ffn(x2d, row(ffn_norm_g[1]), shift, scale, gate, ffn_w_in[1].astype(BF16),
               ffn_w_out[1].astype(BF16), seq=seq, tm=tm,
               attn=(o.reshape(t, aw), attn_w_o[0].astype(BF16), gate_m),
               final_g=row(final_norm_g))
    return out.reshape(batch, seq, d)
```
